```python
import jax, jax.numpy as jnp
from jax import lax
import numpy as np

D_MODEL = 2048
BATCH = 8
SEQ = 2048
DEPTH = 1

D_MIX = D_MODEL
D_A = D_MIX // 2
D_B = D_MIX - D_A
HEAD_A = 128
N_HEADS_A = D_A // HEAD_A
GMLP_CHUNK = 128
HEAD_B_V = 128
N_HEADS_B = D_B // HEAD_B_V
HEAD_B_K = 128
F_B = N_HEADS_B * HEAD_B_K
SCAN_CHUNK = 64
IN_COLS = 2 * D_A + 3 * F_B + 2 * D_B
N_EXPERTS = 32
TOP_K = 4
D_EXPERT = D_MODEL
SWIGLU_LIMIT = 7.0
SWIGLU_ALPHA = 1.702
MOE_BLOCK = 512
EPS = 1e-5

kernel_name = "hybrid_gmlp_hgrn2_moe_encoder"


def rmsnorm(x, w):
    xf = x.astype(jnp.float32)
    xf = xf * lax.rsqrt(jnp.mean(xf * xf, axis=-1, keepdims=True) + EPS)
    return xf.astype(x.dtype) * w


def layernorm(x, w, b):
    xf = x.astype(jnp.float32)
    mu = jnp.mean(xf, axis=-1, keepdims=True)
    var = jnp.mean(jnp.square(xf - mu), axis=-1, keepdims=True)
    return ((xf - mu) * lax.rsqrt(var + EPS)).astype(x.dtype) * w + b


def gmlp_mixer(u, v, ln_w, ln_b, w_s, b_s):
    B, S, _ = v.shape
    nc = S // GMLP_CHUNK
    v = layernorm(v, ln_w, ln_b)
    vh = v.reshape(B, nc, GMLP_CHUNK, N_HEADS_A, HEAD_A)
    mixed = jnp.einsum('hts,bcshd->bcthd', w_s, vh) + b_s.T[None, None, :, :, None]
    return u * mixed.reshape(B, S, D_A)


def gla_chunked(q, k, v, log_f):
    B, S, H, K = q.shape
    V = v.shape[-1]
    C = SCAN_CHUNK
    nc = S // C

    def to_chunks(t):
        return t.astype(jnp.float32).reshape(B, nc, C, H, t.shape[-1]).transpose(1, 0, 3, 2, 4)

    qc, kc, vc, gc = to_chunks(q), to_chunks(k), to_chunks(v), to_chunks(log_f)
    cum = jnp.cumsum(gc, axis=3)
    cum_last = cum[:, :, :, -1:, :]
    q_dec = qc * jnp.exp(cum)
    k_dec = kc * jnp.exp(-cum)
    k_tail = kc * jnp.exp(cum_last - cum)
    mask = jnp.tril(jnp.ones((C, C), dtype=bool))
    scores = jnp.where(mask, jnp.einsum('nbhtk,nbhsk->nbhts', q_dec, k_dec), 0.0)
    o_intra = jnp.einsum('nbhts,nbhsv->nbhtv', scores, vc)
    d_state = jnp.einsum('nbhsk,nbhsv->nbhkv', k_tail, vc)
    decay = jnp.exp(cum_last[:, :, :, 0, :])

    def step(s_prev, inp):
        d, ds = inp
        return d[..., None] * s_prev + ds, s_prev

    s0 = jnp.zeros((B, H, K, V), jnp.float32)
    _, s_before = lax.scan(step, s0, (decay, d_state))
    o = o_intra + jnp.einsum('nbhtk,nbhkv->nbhtv', q_dec, s_before)
    return o.transpose(1, 0, 3, 2, 4).reshape(B, S, H, V)


def lower_bound(table, layer):
    return jnp.cumsum(jax.nn.softmax(table.astype(jnp.float32), axis=0), axis=0)[layer]


def hgrn2_mixer(q_raw, ff_raw, fb_raw, i_raw, g_raw, lb_f, lb_b, norm_w):
    B, S, _ = q_raw.shape
    q = jax.nn.silu(q_raw).reshape(B, S, N_HEADS_B, HEAD_B_K)
    i = i_raw.reshape(B, S, N_HEADS_B, HEAD_B_V)

    def gates(f_raw, lb):
        lb = lb.reshape(N_HEADS_B, HEAD_B_K)
        f = lb + (1.0 - lb) * jax.nn.sigmoid(f_raw.astype(jnp.float32).reshape(B, S, N_HEADS_B, HEAD_B_K))
        return 1.0 - f, jnp.log(f)

    k_f, logf_f = gates(ff_raw, lb_f)
    k_b, logf_b = gates(fb_raw, lb_b)
    o_fwd = gla_chunked(q, k_f, i, logf_f)
    flip = lambda t: jnp.flip(t, axis=1)
    o_bwd = flip(gla_chunked(flip(q), flip(k_b), flip(i), flip(logf_b)))
    o = o_fwd + o_bwd
    o = o * lax.rsqrt(jnp.mean(o * o, axis=-1, keepdims=True) + EPS)
    g = jax.nn.silu(g_raw.astype(jnp.float32)).reshape(B, S, N_HEADS_B, HEAD_B_V)
    o = o.astype(q_raw.dtype) * norm_w * g.astype(q_raw.dtype)
    return o.reshape(B, S, D_B)


def moe(x2d, w_router, b_router, w1, b1, w2, b2):
    N, D = x2d.shape
    logits = (x2d @ w_router + b_router).astype(jnp.float32)
    top_vals, top_idx = lax.top_k(logits, TOP_K)
    gates = jax.nn.softmax(top_vals, axis=-1)
    A = N * TOP_K
    flat_e = top_idx.reshape(A)
    flat_tok = jnp.arange(A, dtype=jnp.int32) // TOP_K
    flat_g = gates.reshape(A)
    order = jnp.argsort(flat_e)
    sorted_e = flat_e[order]
    counts = jnp.bincount(flat_e, length=N_EXPERTS)
    padded = (counts + MOE_BLOCK - 1) // MOE_BLOCK * MOE_BLOCK
    start_sorted = jnp.cumsum(counts) - counts
    pad_end = jnp.cumsum(padded)
    start_pad = pad_end - padded
    dest = start_pad[sorted_e] + (jnp.arange(A, dtype=jnp.int32) - start_sorted[sorted_e])
    n_blocks = -(-A // MOE_BLOCK) + N_EXPERTS
    P = n_blocks * MOE_BLOCK
    row_tok = jnp.zeros((P,), jnp.int32).at[dest].set(flat_tok[order])
    row_gate = jnp.zeros((P,), jnp.float32).at[dest].set(flat_g[order])
    block_start = jnp.arange(n_blocks, dtype=jnp.int32) * MOE_BLOCK
    block_e = jnp.minimum(jnp.searchsorted(pad_end, block_start, side='right'), N_EXPERTS - 1)

    def run_block(args):
        e, toks, g = args
        h = x2d[toks] @ w1[e] + b1[e]
        x_glu = jnp.minimum(h[:, ::2], SWIGLU_LIMIT)
        x_lin = jnp.clip(h[:, 1::2], -SWIGLU_LIMIT, SWIGLU_LIMIT)
        act = x_glu * jax.nn.sigmoid(SWIGLU_ALPHA * x_glu) * (x_lin + 1.0)
        y = act @ w2[e] + b2[e]
        return y * g[:, None].astype(y.dtype)

    y = lax.map(run_block, (block_e, row_tok.reshape(n_blocks, MOE_BLOCK),
                            row_gate.reshape(n_blocks, MOE_BLOCK)))
    return jax.ops.segment_sum(y.reshape(P, D), row_tok, num_segments=N)


def setup_inputs(seed: int = 0) -> dict:
    key = jax.random.key(seed)
    ks = jax.random.split(key, 24)
    f32 = jnp.float32
    nrm = lambda k, shape, scale: jax.random.normal(k, shape, f32) * scale
    gain = lambda k, shape: 1.0 + 0.02 * jax.random.normal(k, shape, f32)
    return {
        "x": nrm(ks[0], (BATCH, SEQ, D_MODEL), 1.0),
        "mix_norm": gain(ks[1], (DEPTH, D_MODEL)),
        "w_in": nrm(ks[2], (DEPTH, D_MODEL, IN_COLS), D_MODEL ** -0.5),
        "ln_v_w": gain(ks[3], (DEPTH, D_A)),
        "ln_v_b": nrm(ks[4], (DEPTH, D_A), 0.02),
        "w_spatial": nrm(ks[5], (DEPTH, N_HEADS_A, GMLP_CHUNK, GMLP_CHUNK), GMLP_CHUNK ** -0.5),
        "b_spatial": gain(ks[6], (DEPTH, N_HEADS_A, GMLP_CHUNK)),
        "gmlp_out_norm": gain(ks[7], (DEPTH, D_A)),
        "hgrn_lb_fwd": nrm(ks[8], (DEPTH + 1, F_B), 0.1),
        "hgrn_lb_bwd": nrm(ks[9], (DEPTH + 1, F_B), 0.1),
        "hgrn_out_norm": gain(ks[10], (DEPTH, HEAD_B_V)),
        "w_out": nrm(ks[11], (DEPTH, D_MIX, D_MODEL), D_MIX ** -0.5),
        "moe_norm": gain(ks[12], (DEPTH, D_MODEL)),
        "w_router": nrm(ks[13], (DEPTH, D_MODEL, N_EXPERTS), D_MODEL ** -0.5),
        "b_router": nrm(ks[14], (DEPTH, N_EXPERTS), 0.01),
        "w1": nrm(ks[15], (DEPTH, N_EXPERTS, D_MODEL, 2 * D_EXPERT), D_MODEL ** -0.5),
        "b1": nrm(ks[16], (DEPTH, N_EXPERTS, 2 * D_EXPERT), 0.01),
        "w2": nrm(ks[17], (DEPTH, N_EXPERTS, D_EXPERT, D_MODEL), D_EXPERT ** -0.5),
        "b2": nrm(ks[18], (DEPTH, N_EXPERTS, D_MODEL), 0.01),
        "final_norm": gain(ks[19], (D_MODEL,)),
    }


def reference(x, mix_norm, w_in, ln_v_w, ln_v_b, w_spatial, b_spatial, gmlp_out_norm,
              hgrn_lb_fwd, hgrn_lb_bwd, hgrn_out_norm, w_out, moe_norm, w_router, b_router,
              w1, b1, w2, b2, final_norm):
    B, S, D = x.shape
    split_at = [D_A, 2 * D_A, 2 * D_A + F_B, 2 * D_A + 2 * F_B, 2 * D_A + 3 * F_B,
                2 * D_A + 3 * F_B + D_B]
    for l in range(DEPTH):
        h = rmsnorm(x, mix_norm[l])
        z = h @ w_in[l]
        u, v, q_raw, ff_raw, fb_raw, i_raw, g_raw = jnp.split(z, split_at, axis=-1)
        a = gmlp_mixer(jax.nn.gelu(u, approximate=False), jax.nn.gelu(v, approximate=False),
                       ln_v_w[l], ln_v_b[l], w_spatial[l], b_spatial[l])
        a = rmsnorm(a, gmlp_out_norm[l])
        bo = hgrn2_mixer(q_raw, ff_raw, fb_raw, i_raw, g_raw,
                         lower_bound(hgrn_lb_fwd, l), lower_bound(hgrn_lb_bwd, l), hgrn_out_norm[l])
        x = x + jnp.concatenate([a, bo], axis=-1) @ w_out[l]
        h2 = rmsnorm(x, moe_norm[l]).reshape(B * S, D)
        x = x + moe(h2, w_router[l], b_router[l], w1[l], b1[l], w2[l], b2[l]).reshape(B, S, D)
    return rmsnorm(x, final_norm)
```

```python
import functools

import jax
import jax.numpy as jnp
from jax import lax
from jax.experimental import pallas as pl
from jax.experimental.pallas import tpu as pltpu

EPS = 1e-5
LANES = 128
GMLP_CHUNK = 128
SCAN_CHUNK = 64
TOP_K = 4
SWIGLU_LIMIT = 7.0
SWIGLU_ALPHA = 1.702
VMEM_LIMIT = 56 * 1024 * 1024

_F32 = jnp.float32
_BF16 = jnp.bfloat16


def _tile(n, target, align):
    if n <= target:
        return n
    t = target - target % align
    while t > align and n % t:
        t -= align
    assert n % t == 0, (n, target, align)
    return t


def _params(*semantics):
    return pltpu.CompilerParams(dimension_semantics=semantics, vmem_limit_bytes=VMEM_LIMIT)


def _dot(a, b):
    return jnp.dot(a, b, preferred_element_type=_F32)


def _dot_nt(a, b):
    return lax.dot_general(a, b, (((1,), (1,)), ((), ())), preferred_element_type=_F32)


def _dot_tn(a, b):
    return lax.dot_general(a, b, (((0,), (0,)), ((), ())), preferred_element_type=_F32)


def _rms_scale(x):
    return x * lax.rsqrt(jnp.mean(x * x, axis=-1, keepdims=True) + EPS)


def _sigmoid(x):
    return jax.nn.sigmoid(x)


def _inproj_body(x_ref, g_ref, w_ref, o_ref, h_scr):
    @pl.when(pl.program_id(1) == 0)
    def _():
        h_scr[...] = (_rms_scale(x_ref[...]) * g_ref[...]).astype(h_scr.dtype)

    o_ref[...] = _dot(h_scr[...], w_ref[...]).astype(o_ref.dtype)


def _in_proj(x2, gain, w_bf):
    n, d = x2.shape
    c = w_bf.shape[1]
    tm = _tile(n, 1024, 8)
    tn = _tile(c, 1024, LANES)
    return pl.pallas_call(
        _inproj_body,
        grid=(n // tm, c // tn),
        in_specs=[pl.BlockSpec((tm, d), lambda i, j: (i, 0)),
                  pl.BlockSpec((1, d), lambda i, j: (0, 0)),
                  pl.BlockSpec((d, tn), lambda i, j: (0, j))],
        out_specs=pl.BlockSpec((tm, tn), lambda i, j: (i, j)),
        out_shape=jax.ShapeDtypeStruct((n, c), _BF16),
        scratch_shapes=[pltpu.VMEM((tm, d), _BF16)],
        compiler_params=_params("parallel", "arbitrary"),
        name="in_proj",
    )(x2, gain, w_bf)


def _gelu(x):
    return 0.5 * x * (1.0 + lax.erf(x * (2.0 ** -0.5)))


def _gmlp_body(u_ref, v_ref, lnw_ref, lnb_ref, ws_ref, bs_ref, gw_ref, o_ref, a_scr, *, n_heads):
    c = GMLP_CHUNK
    d_a = n_heads * LANES

    def chunk(ci, carry):
        r0 = pl.multiple_of(ci * c, c)
        u = _gelu(u_ref[pl.ds(r0, c), :].astype(_F32))
        v = _gelu(v_ref[pl.ds(r0, c), :].astype(_F32))
        mu = jnp.mean(v, axis=-1, keepdims=True)
        vc = v - mu
        var = jnp.mean(vc * vc, axis=-1, keepdims=True)
        vn = ((vc * lax.rsqrt(var + EPS)) * lnw_ref[...] + lnb_ref[...]).astype(_BF16)
        ss = jnp.zeros((c, 1), _F32)
        for h in range(n_heads):
            cols = slice(h * LANES, (h + 1) * LANES)
            mixed = _dot(ws_ref[h], vn[:, cols]) + bs_ref[h]
            a = u[:, cols] * mixed
            ss = ss + jnp.sum(a * a, axis=-1, keepdims=True)
            a_scr[:, cols] = a
        scale = lax.rsqrt(ss * (1.0 / d_a) + EPS)
        o_ref[pl.ds(r0, c), :] = (a_scr[...] * scale * gw_ref[...]).astype(o_ref.dtype)
        return carry

    lax.fori_loop(0, u_ref.shape[0] // c, chunk, 0)


def _gmlp(z, ln_w, ln_b, ws_bf, bs_col, out_gain, d_a):
    n = z.shape[0]
    n_heads = d_a // LANES
    rows = _tile(n, 512, GMLP_CHUNK)
    full = lambda shape: pl.BlockSpec(shape, lambda i: (0,) * len(shape))
    return pl.pallas_call(
        functools.partial(_gmlp_body, n_heads=n_heads),
        grid=(n // rows,),
        in_specs=[pl.BlockSpec((rows, d_a), lambda i: (i, 0)),
                  pl.BlockSpec((rows, d_a), lambda i: (i, 1)),
                  full((1, d_a)), full((1, d_a)),
                  full((n_heads, GMLP_CHUNK, GMLP_CHUNK)),
                  full((n_heads, GMLP_CHUNK, 1)),
                  full((1, d_a))],
        out_specs=pl.BlockSpec((rows, d_a), lambda i: (i, 0)),
        out_shape=jax.ShapeDtypeStruct((n, d_a), _BF16),
        scratch_shapes=[pltpu.VMEM((GMLP_CHUNK, d_a), _F32)],
        compiler_params=_params("parallel"),
        name="gmlp",
    )(z, z, ln_w, ln_b, ws_bf, bs_col, out_gain)


def _hgrn_body(q_ref, ff_ref, fb_ref, i_ref, g_ref, lbf_ref, lbb_ref, nw_ref, o_ref,
               of_scr, ob_scr, sf_scr, sb_scr, *, layer):
    c = SCAN_CHUNK
    seq = q_ref.shape[0]
    n_chunks = seq // c

    def lower_bound(t_ref):
        t = t_ref[...].astype(_F32)
        e = jnp.exp(t - jnp.max(t, axis=0, keepdims=True))
        sm = e / jnp.sum(e, axis=0, keepdims=True)
        return jnp.sum(sm[:layer + 1], axis=0, keepdims=True)

    lb_f = lower_bound(lbf_ref)
    lb_b = lower_bound(lbb_ref)
    sf_scr[...] = jnp.zeros_like(sf_scr)
    sb_scr[...] = jnp.zeros_like(sb_scr)

    row = lax.broadcasted_iota(jnp.int32, (c, LANES), 0)
    ti = lax.broadcasted_iota(jnp.int32, (c, c), 0)
    si = lax.broadcasted_iota(jnp.int32, (c, c), 1)

    def scan(x, reverse):
        d = 1
        while d < c:
            if reverse:
                x = x + jnp.where(row < c - d, pltpu.roll(x, c - d, 0), 0.0)
            else:
                x = x + jnp.where(row >= d, pltpu.roll(x, d, 0), 0.0)
            d *= 2
        return x

    def one_dir(r0, f_ref, lb, s_scr, out_scr, reverse):
        q = q_ref[pl.ds(r0, c), :].astype(_F32)
        q = q * _sigmoid(q)
        f = lb + (1.0 - lb) * _sigmoid(f_ref[pl.ds(r0, c), :].astype(_F32))
        k = 1.0 - f
        cum = scan(jnp.log(f), reverse)
        tot = cum[0:1, :] if reverse else cum[c - 1:c, :]
        qd = (q * jnp.exp(cum)).astype(_BF16)
        kd = (k * jnp.exp(-cum)).astype(_BF16)
        kt = (k * jnp.exp(tot - cum)).astype(_BF16)
        v = i_ref[pl.ds(r0, c), :]
        mask = (si >= ti) if reverse else (si <= ti)
        sc = jnp.where(mask, _dot_nt(qd, kd), 0.0).astype(_BF16)
        s_prev = s_scr[...]
        out_scr[pl.ds(r0, c), :] = _dot(sc, v) + _dot_nt(qd, s_prev.astype(_BF16))
        s_scr[...] = s_prev * jnp.exp(tot) + _dot_tn(v, kt)

    def step(ci, carry):
        one_dir(pl.multiple_of(ci * c, c), ff_ref, lb_f, sf_scr, of_scr, False)
        one_dir(pl.multiple_of((n_chunks - 1 - ci) * c, c), fb_ref, lb_b, sb_scr, ob_scr, True)
        return carry

    lax.fori_loop(0, n_chunks, step, 0)

    rb = _tile(seq, 256, c)

    def finish(bi, carry):
        r0 = pl.multiple_of(bi * rb, rb)
        o = _rms_scale(of_scr[pl.ds(r0, rb), :] + ob_scr[pl.ds(r0, rb), :])
        g = g_ref[pl.ds(r0, rb), :].astype(_F32)
        o_ref[pl.ds(r0, rb), :] = (o * nw_ref[...] * (g * _sigmoid(g))).astype(o_ref.dtype)
        return carry

    lax.fori_loop(0, seq // rb, finish, 0)


def _hgrn(z, lb_f, lb_b, norm_w, batch, seq, col0, f_b, d_b, layer):
    n_heads = d_b // LANES
    h0 = col0 // LANES
    kh = f_b // LANES
    zspec = lambda off: pl.BlockSpec((seq, LANES), lambda b, h: (b, off + h))
    depth1 = lb_f.shape[0]
    return pl.pallas_call(
        functools.partial(_hgrn_body, layer=layer),
        grid=(batch, n_heads),
        in_specs=[zspec(h0), zspec(h0 + kh), zspec(h0 + 2 * kh), zspec(h0 + 3 * kh),
                  zspec(h0 + 3 * kh + n_heads),
                  pl.BlockSpec((depth1, LANES), lambda b, h: (0, h)),
                  pl.BlockSpec((depth1, LANES), lambda b, h: (0, h)),
                  pl.BlockSpec((1, LANES), lambda b, h: (0, 0))],
        out_specs=pl.BlockSpec((seq, LANES), lambda b, h: (b, h)),
        out_shape=jax.ShapeDtypeStruct((batch * seq, d_b), _BF16),
        scratch_shapes=[pltpu.VMEM((seq, LANES), _F32), pltpu.VMEM((seq, LANES), _F32),
                        pltpu.VMEM((LANES, LANES), _F32), pltpu.VMEM((LANES, LANES), _F32)],
        compiler_params=_params("parallel", "parallel"),
        name="hgrn2",
    )(z, z, z, z, z, lb_f, lb_b, norm_w)


def _outproj_body(a_ref, b_ref, wa_ref, wb_ref, x_ref, mg_ref, wr_ref, br_ref,
                  x1_ref, h2_ref, lg_ref):
    x1 = x_ref[...] + _dot(a_ref[...], wa_ref[...]) + _dot(b_ref[...], wb_ref[...])
    x1_ref[...] = x1
    h2 = _rms_scale(x1) * mg_ref[...]
    h2_ref[...] = h2
    lg_ref[...] = _dot(h2.astype(_BF16), wr_ref[...]) + br_ref[...]


def _out_proj(a, b, wa_bf, wb_bf, x2, moe_gain, wr_bf, b_router):
    n, d = x2.shape
    d_a, d_b, e = a.shape[1], b.shape[1], wr_bf.shape[1]
    tm = _tile(n, 512, 8)
    full = lambda shape: pl.BlockSpec(shape, lambda i: (0,) * len(shape))
    row = lambda w: pl.BlockSpec((tm, w), lambda i: (i, 0))
    return pl.pallas_call(
        _outproj_body,
        grid=(n // tm,),
        in_specs=[row(d_a), row(d_b), full((d_a, d)), full((d_b, d)), row(d),
                  full((1, d)), full((d, e)), full((1, e))],
        out_specs=[row(d), row(d), row(e)],
        out_shape=[jax.ShapeDtypeStruct((n, d), _F32), jax.ShapeDtypeStruct((n, d), _F32),
                   jax.ShapeDtypeStruct((n, e), _F32)],
        compiler_params=_params("parallel"),
        name="out_proj",
    )(a, b, wa_bf, wb_bf, x2, moe_gain, wr_bf, b_router)


def _route_body(lg_ref, idx_ref, gate_ref, rank_ref, cnt_ref, carry_scr):
    @pl.when(pl.program_id(0) == 0)
    def _():
        carry_scr[...] = jnp.zeros_like(carry_scr)

    l = lg_ref[...]
    tm, e = l.shape
    lane = lax.broadcasted_iota(jnp.int32, (tm, e), 1).astype(_F32)
    vals, idxs, hots = [], [], []
    for _ in range(TOP_K):
        m = jnp.max(l, axis=-1, keepdims=True)
        ix = jnp.min(jnp.where(l == m, lane, float(e)), axis=-1, keepdims=True)
        hot = lane == ix
        vals.append(m)
        idxs.append(ix)
        hots.append(hot)
        l = jnp.where(hot, -jnp.inf, l)
    exps = [jnp.exp(v - vals[0]) for v in vals]
    denom = functools.reduce(lambda p, q: p + q, exps)
    sel = functools.reduce(lambda p, q: p + q, [h.astype(_F32) for h in hots])
    ti = lax.broadcasted_iota(jnp.int32, (tm, tm), 0)
    si = lax.broadcasted_iota(jnp.int32, (tm, tm), 1)
    before = _dot((si < ti).astype(_BF16), sel.astype(_BF16)) + carry_scr[...]
    slot = lax.broadcasted_iota(jnp.int32, (tm, TOP_K), 1)
    idx_out = jnp.zeros((tm, TOP_K), _F32)
    gate_out = jnp.zeros((tm, TOP_K), _F32)
    rank_out = jnp.zeros((tm, TOP_K), _F32)
    for k in range(TOP_K):
        rank_k = jnp.sum(jnp.where(hots[k], before, 0.0), axis=-1, keepdims=True)
        idx_out = jnp.where(slot == k, idxs[k], idx_out)
        gate_out = jnp.where(slot == k, exps[k] / denom, gate_out)
        rank_out = jnp.where(slot == k, rank_k, rank_out)
    idx_ref[...] = idx_out.astype(jnp.int32)
    gate_ref[...] = gate_out
    rank_ref[...] = rank_out.astype(jnp.int32)
    carry_scr[...] += jnp.sum(sel, axis=0, keepdims=True)
    cnt_ref[...] = carry_scr[...].astype(jnp.int32)


def _route(logits):
    n, e = logits.shape
    tm = _tile(n, 512, 8)
    slot = pl.BlockSpec((tm, TOP_K), lambda i: (i, 0))
    return pl.pallas_call(
        _route_body,
        grid=(n // tm,),
        in_specs=[pl.BlockSpec((tm, e), lambda i: (i, 0))],
        out_specs=[slot, slot, slot, pl.BlockSpec((1, e), lambda i: (0, 0))],
        out_shape=[jax.ShapeDtypeStruct((n, TOP_K), jnp.int32), jax.ShapeDtypeStruct((n, TOP_K), _F32),
                   jax.ShapeDtypeStruct((n, TOP_K), jnp.int32), jax.ShapeDtypeStruct((1, e), jnp.int32)],
        scratch_shapes=[pltpu.VMEM((1, e), _F32)],
        compiler_params=_params("arbitrary"),
        name="route",
    )(logits)


def _moe_body(be_ref, nv_ref, tok_ref, h2_hbm, w1_ref, b1_ref, w2_ref, b2_ref, y_ref,
              x_scr, xb_scr, acc_scr, w2p_scr, sem):
    i = pl.program_id(0)
    j = pl.program_id(1)
    n_j = pl.num_programs(1)
    bm, d = x_scr.shape
    tf = w2_ref.shape[1]
    valid = i < nv_ref[0]

    @pl.when(jnp.logical_and(valid, j == 0))
    def _():
        def issue(r, carry):
            pltpu.make_async_copy(h2_hbm.at[pl.ds(tok_ref[r], 1), :], x_scr.at[pl.ds(r, 1), :], sem).start()
            return carry

        lax.fori_loop(0, bm, issue, 0)
        pltpu.make_async_copy(h2_hbm.at[pl.ds(0, bm), :], x_scr, sem).wait()
        xb_scr[...] = x_scr[...].astype(_BF16)
        acc_scr[...] = jnp.zeros_like(acc_scr)

    @pl.when(valid)
    def _():
        h = _dot(xb_scr[...], w1_ref[0].astype(_BF16)) + b1_ref[0]
        even = (lax.broadcasted_iota(jnp.int32, (bm, LANES), 1) & 1) == 0
        half = LANES // 2
        acts = []
        for g in range(tf // LANES):
            pa = h[:, (2 * g) * LANES:(2 * g + 1) * LANES]
            pb = h[:, (2 * g + 1) * LANES:(2 * g + 2) * LANES]
            glu = jnp.where(even, pa, pltpu.roll(pb, 1, 1))
            lin = jnp.where(even, pltpu.roll(pa, LANES - 1, 1), pb)
            glu = jnp.minimum(glu, SWIGLU_LIMIT)
            lin = jnp.clip(lin, -SWIGLU_LIMIT, SWIGLU_LIMIT)
            acts.append((glu * _sigmoid(SWIGLU_ALPHA * glu) * (lin + 1.0)).astype(_BF16))
            base = g * LANES
            for cb in range(d // LANES):
                cols = slice(cb * LANES, (cb + 1) * LANES)
                w2p_scr[cb, pl.ds(base, half, stride=2), :] = w2_ref[0, pl.ds(base, half), cols]
                w2p_scr[cb, pl.ds(base + 1, half, stride=2), :] = w2_ref[0, pl.ds(base + half, half), cols]
        act = jnp.concatenate(acts, axis=1) if len(acts) > 1 else acts[0]
        w2p = jnp.concatenate([w2p_scr[cb] for cb in range(d // LANES)], axis=1).astype(_BF16)
        acc_scr[...] += _dot(act, w2p)

    @pl.when(j == n_j - 1)
    def _():
        y_ref[...] = jnp.where(valid, acc_scr[...] + b2_ref[0], 0.0)


def _moe_block_rows(assignments, n_experts):
    bm = 128
    while bm < 512 and bm * 2 * 4 <= assignments // n_experts:
        bm *= 2
    return bm


def _moe(h2, row_tok, block_e, n_valid, w1, b1, w2, b2, bm):
    n, d = h2.shape
    e, _, f2 = w1.shape
    f = f2 // 2
    p = row_tok.shape[0]
    n_blocks = p // bm
    tf = _tile(f, 256, LANES)
    n_j = f // tf

    def blk(i, nv):
        return jnp.minimum(i, nv[0] - 1)

    def jj(i, j, nv):
        return jnp.where(i < nv[0], j, n_j - 1)

    grid_spec = pltpu.PrefetchScalarGridSpec(
        num_scalar_prefetch=2,
        grid=(n_blocks, n_j),
        in_specs=[
            pl.BlockSpec((bm,), lambda i, j, be, nv: (blk(i, nv),), memory_space=pltpu.SMEM),
            pl.BlockSpec(memory_space=pl.ANY),
            pl.BlockSpec((1, d, 2 * tf), lambda i, j, be, nv: (be[blk(i, nv)], 0, jj(i, j, nv))),
            pl.BlockSpec((1, 1, 2 * tf), lambda i, j, be, nv: (be[blk(i, nv)], 0, jj(i, j, nv))),
            pl.BlockSpec((1, tf, d), lambda i, j, be, nv: (be[blk(i, nv)], jj(i, j, nv), 0)),
            pl.BlockSpec((1, 1, d), lambda i, j, be, nv: (be[blk(i, nv)], 0, 0)),
        ],
        out_specs=pl.BlockSpec((bm, d), lambda i, j, be, nv: (i, 0)),
        scratch_shapes=[pltpu.VMEM((bm, d), _F32), pltpu.VMEM((bm, d), _BF16), pltpu.VMEM((bm, d), _F32),
                        pltpu.VMEM((d // LANES, tf, LANES), _F32), pltpu.SemaphoreType.DMA(())],
    )
    return pl.pallas_call(
        _moe_body,
        grid_spec=grid_spec,
        out_shape=jax.ShapeDtypeStruct((p, d), _F32),
        compiler_params=_params("arbitrary", "arbitrary"),
        name="moe",
    )(block_e, n_valid, row_tok, h2, w1, b1.reshape(e, 1, f2), w2, b2.reshape(e, 1, d))


def _combine_body(dest_ref, gate_ref, x1_ref, fw_ref, y_hbm, o_ref, y_scr, sem):
    tm, d = x1_ref.shape

    def issue(t, carry):
        for k in range(TOP_K):
            pltpu.make_async_copy(y_hbm.at[pl.ds(dest_ref[t * TOP_K + k], 1), :],
                                  y_scr.at[pl.ds(k * tm + t, 1), :], sem).start()
        return carry

    lax.fori_loop(0, tm, issue, 0)
    pltpu.make_async_copy(y_hbm.at[pl.ds(0, TOP_K * tm), :], y_scr, sem).wait()
    acc = x1_ref[...]
    gates = gate_ref[...]
    for k in range(TOP_K):
        acc = acc + y_scr[pl.ds(k * tm, tm), :] * gates[:, k:k + 1]
    o_ref[...] = _rms_scale(acc) * fw_ref[...]


def _combine(dest_flat, gates, x1, final_gain, y):
    n, d = x1.shape
    tm = _tile(n, 256, 8)
    return pl.pallas_call(
        _combine_body,
        grid=(n // tm,),
        in_specs=[pl.BlockSpec((tm * TOP_K,), lambda i: (i,), memory_space=pltpu.SMEM),
                  pl.BlockSpec((tm, TOP_K), lambda i: (i, 0)),
                  pl.BlockSpec((tm, d), lambda i: (i, 0)),
                  pl.BlockSpec((1, d), lambda i: (0, 0)),
                  pl.BlockSpec(memory_space=pl.ANY)],
        out_specs=pl.BlockSpec((tm, d), lambda i: (i, 0)),
        out_shape=jax.ShapeDtypeStruct((n, d), _F32),
        scratch_shapes=[pltpu.VMEM((TOP_K * tm, d), _F32), pltpu.SemaphoreType.DMA(())],
        compiler_params=_params("arbitrary"),
        name="combine",
    )(dest_flat, gates, x1, final_gain, y)


def kernel(x, mix_norm, w_in, ln_v_w, ln_v_b, w_spatial, b_spatial, gmlp_out_norm, hgrn_lb_fwd,
           hgrn_lb_bwd, hgrn_out_norm, w_out, moe_norm, w_router, b_router, w1, b1, w2, b2, final_norm):
    batch, seq, d = x.shape
    depth = mix_norm.shape[0]
    assert depth == 1, "single trunk layer"
    layer = 0
    d_a = ln_v_w.shape[1]
    d_b = w_out.shape[1] - d_a
    f_b = hgrn_lb_fwd.shape[1]
    n_experts = w_router.shape[2]
    n = batch * seq
    assert seq % GMLP_CHUNK == 0 and seq % SCAN_CHUNK == 0 and d_a % LANES == 0 and d_a == d_b == f_b
    assert hgrn_out_norm.shape[1] == LANES and w_spatial.shape[2] == GMLP_CHUNK

    x2 = x.reshape(n, d)
    z = _in_proj(x2, mix_norm[layer][None], w_in[layer].astype(_BF16))
    a = _gmlp(z, ln_v_w[layer][None], ln_v_b[layer][None], w_spatial[layer].astype(_BF16),
              b_spatial[layer][:, :, None], gmlp_out_norm[layer][None], d_a)
    bo = _hgrn(z, hgrn_lb_fwd, hgrn_lb_bwd, hgrn_out_norm[layer][None], batch, seq, 2 * d_a, f_b, d_b, layer)
    w_out_bf = w_out[layer].astype(_BF16)
    x1, h2, logits = _out_proj(a, bo, w_out_bf[:d_a], w_out_bf[d_a:], x2, moe_norm[layer][None],
                               w_router[layer].astype(_BF16), b_router[layer][None])

    idx, gates, rank, counts = _route(logits)

    assignments = n * TOP_K
    bm = _moe_block_rows(assignments, n_experts)
    n_blocks = -(-assignments // bm) + n_experts
    counts = counts[0]
    padded = (counts + bm - 1) // bm * bm
    pad_end = jnp.cumsum(padded)
    start_pad = pad_end - padded
    dest = (start_pad[idx] + rank).astype(jnp.int32)
    tok = jnp.broadcast_to(jnp.arange(n, dtype=jnp.int32)[:, None], (n, TOP_K))
    row_tok = jnp.zeros((n_blocks * bm,), jnp.int32).at[dest.reshape(-1)].set(tok.reshape(-1))
    block_start = jnp.arange(n_blocks, dtype=jnp.int32) * bm
    block_e = jnp.minimum(jnp.searchsorted(pad_end, block_start, side='right'), n_experts - 1).astype(jnp.int32)
    n_valid = (pad_end[-1:] // bm).astype(jnp.int32)

    y = _moe(h2, row_tok, block_e, n_valid, w1[layer], b1[layer], w2[layer], b2[layer], bm)
    out = _combine(dest.reshape(-1), gates, x1, final_norm[None], y)
    return out.reshape(batch, seq, d)
```

```python
import functools

import jax
import jax.numpy as jnp
from jax import lax
from jax.experimental import pallas as pl
from jax.experimental.pallas import tpu as pltpu

EPS = 1e-5
LANES = 128
GMLP_CHUNK = 128
SCAN_CHUNK = 64
TOP_K = 4
SWIGLU_LIMIT = 7.0
SWIGLU_ALPHA = 1.702
VMEM_LIMIT = 56 * 1024 * 1024

_F32 = jnp.float32
_BF16 = jnp.bfloat16


def _tile(n, target, align):
    if n <= target:
        return n
    t = target - target % align
    while t > align and n % t:
        t -= align
    assert n % t == 0, (n, target, align)
    return t


def _params(*semantics, row_dma_loop=False):
    return pltpu.CompilerParams(dimension_semantics=semantics, vmem_limit_bytes=VMEM_LIMIT,
                                disable_bounds_checks=row_dma_loop)


def _dot(a, b):
    return jnp.dot(a, b, preferred_element_type=_F32)


def _dot_nt(a, b):
    return lax.dot_general(a, b, (((1,), (1,)), ((), ())), preferred_element_type=_F32)


def _dot_tn(a, b):
    return lax.dot_general(a, b, (((0,), (0,)), ((), ())), preferred_element_type=_F32)


def _rms_scale(x):
    return x * lax.rsqrt(jnp.mean(x * x, axis=-1, keepdims=True) + EPS)


def _sigmoid(x):
    return jax.nn.sigmoid(x)


def _inproj_body(x_ref, g_ref, w_ref, o_ref, h_scr):
    @pl.when(pl.program_id(1) == 0)
    def _():
        h_scr[...] = (_rms_scale(x_ref[...]) * g_ref[...]).astype(h_scr.dtype)

    o_ref[...] = _dot(h_scr[...], w_ref[...]).astype(o_ref.dtype)


def _in_proj(x2, gain, w_bf):
    n, d = x2.shape
    c = w_bf.shape[1]
    tm = _tile(n, 1024, 8)
    tn = _tile(c, 1024, LANES)
    return pl.pallas_call(
        _inproj_body,
        grid=(n // tm, c // tn),
        in_specs=[pl.BlockSpec((tm, d), lambda i, j: (i, 0)),
                  pl.BlockSpec((1, d), lambda i, j: (0, 0)),
                  pl.BlockSpec((d, tn), lambda i, j: (0, j))],
        out_specs=pl.BlockSpec((tm, tn), lambda i, j: (i, j)),
        out_shape=jax.ShapeDtypeStruct((n, c), _BF16),
        scratch_shapes=[pltpu.VMEM((tm, d), _BF16)],
        compiler_params=_params("parallel", "arbitrary"),
        name="in_proj",
    )(x2, gain, w_bf)


def _gelu(x):
    return 0.5 * x * (1.0 + lax.erf(x * (2.0 ** -0.5)))


def _gmlp_body(u_ref, v_ref, lnw_ref, lnb_ref, ws_ref, bs_ref, gw_ref, o_ref, a_scr, *, n_heads):
    c = GMLP_CHUNK
    d_a = n_heads * LANES

    def chunk(ci, carry):
        r0 = pl.multiple_of(ci * c, c)
        u = _gelu(u_ref[pl.ds(r0, c), :].astype(_F32))
        v = _gelu(v_ref[pl.ds(r0, c), :].astype(_F32))
        mu = jnp.mean(v, axis=-1, keepdims=True)
        vc = v - mu
        var = jnp.mean(vc * vc, axis=-1, keepdims=True)
        vn = ((vc * lax.rsqrt(var + EPS)) * lnw_ref[...] + lnb_ref[...]).astype(_BF16)
        ss = jnp.zeros((c, 1), _F32)
        for h in range(n_heads):
            cols = slice(h * LANES, (h + 1) * LANES)
            mixed = _dot(ws_ref[h], vn[:, cols]) + bs_ref[h]
            a = u[:, cols] * mixed
            ss = ss + jnp.sum(a * a, axis=-1, keepdims=True)
            a_scr[:, cols] = a
        scale = lax.rsqrt(ss * (1.0 / d_a) + EPS)
        o_ref[pl.ds(r0, c), :] = (a_scr[...] * scale * gw_ref[...]).astype(o_ref.dtype)
        return carry

    lax.fori_loop(0, u_ref.shape[0] // c, chunk, 0)


def _gmlp(z, ln_w, ln_b, ws_bf, bs_col, out_gain, d_a):
    n = z.shape[0]
    n_heads = d_a // LANES
    rows = _tile(n, 512, GMLP_CHUNK)
    full = lambda shape: pl.BlockSpec(shape, lambda i: (0,) * len(shape))
    return pl.pallas_call(
        functools.partial(_gmlp_body, n_heads=n_heads),
        grid=(n // rows,),
        in_specs=[pl.BlockSpec((rows, d_a), lambda i: (i, 0)),
                  pl.BlockSpec((rows, d_a), lambda i: (i, 1)),
                  full((1, d_a)), full((1, d_a)),
                  full((n_heads, GMLP_CHUNK, GMLP_CHUNK)),
                  full((n_heads, GMLP_CHUNK, 1)),
                  full((1, d_a))],
        out_specs=pl.BlockSpec((rows, d_a), lambda i: (i, 0)),
        out_shape=jax.ShapeDtypeStruct((n, d_a), _BF16),
        scratch_shapes=[pltpu.VMEM((GMLP_CHUNK, d_a), _F32)],
        compiler_params=_params("parallel"),
        name="gmlp",
    )(z, z, ln_w, ln_b, ws_bf, bs_col, out_gain)


def _hgrn_body(q_ref, ff_ref, fb_ref, i_ref, g_ref, lbf_ref, lbb_ref, nw_ref, o_ref,
               of_scr, ob_scr, sf_scr, sb_scr, *, layer):
    c = SCAN_CHUNK
    seq = q_ref.shape[0]
    n_chunks = seq // c

    def lower_bound(t_ref):
        t = t_ref[...].astype(_F32)
        e = jnp.exp(t - jnp.max(t, axis=0, keepdims=True))
        sm = e / jnp.sum(e, axis=0, keepdims=True)
        return jnp.sum(sm[:layer + 1], axis=0, keepdims=True)

    lb_f = lower_bound(lbf_ref)
    lb_b = lower_bound(lbb_ref)
    sf_scr[...] = jnp.zeros_like(sf_scr)
    sb_scr[...] = jnp.zeros_like(sb_scr)

    row = lax.broadcasted_iota(jnp.int32, (c, LANES), 0)
    ti = lax.broadcasted_iota(jnp.int32, (c, c), 0)
    si = lax.broadcasted_iota(jnp.int32, (c, c), 1)

    def scan(x, reverse):
        d = 1
        while d < c:
            if reverse:
                x = x + jnp.where(row < c - d, pltpu.roll(x, c - d, 0), 0.0)
            else:
                x = x + jnp.where(row >= d, pltpu.roll(x, d, 0), 0.0)
            d *= 2
        return x

    def one_dir(r0, f_ref, lb, s_scr, out_scr, reverse):
        q = q_ref[pl.ds(r0, c), :].astype(_F32)
        q = q * _sigmoid(q)
        f = lb + (1.0 - lb) * _sigmoid(f_ref[pl.ds(r0, c), :].astype(_F32))
        k = 1.0 - f
        cum = scan(jnp.log(f), reverse)
        tot = cum[0:1, :] if reverse else cum[c - 1:c, :]
        qd = (q * jnp.exp(cum)).astype(_BF16)
        kd = (k * jnp.exp(-cum)).astype(_BF16)
        kt = (k * jnp.exp(tot - cum)).astype(_BF16)
        v = i_ref[pl.ds(r0, c), :]
        mask = (si >= ti) if reverse else (si <= ti)
        sc = jnp.where(mask, _dot_nt(qd, kd), 0.0).astype(_BF16)
        s_prev = s_scr[...]
        out_scr[pl.ds(r0, c), :] = _dot(sc, v) + _dot_nt(qd, s_prev.astype(_BF16))
        s_scr[...] = s_prev * jnp.exp(tot) + _dot_tn(v, kt)

    def step(ci, carry):
        one_dir(pl.multiple_of(ci * c, c), ff_ref, lb_f, sf_scr, of_scr, False)
        one_dir(pl.multiple_of((n_chunks - 1 - ci) * c, c), fb_ref, lb_b, sb_scr, ob_scr, True)
        return carry

    lax.fori_loop(0, n_chunks, step, 0)

    rb = _tile(seq, 256, c)

    def finish(bi, carry):
        r0 = pl.multiple_of(bi * rb, rb)
        o = _rms_scale(of_scr[pl.ds(r0, rb), :] + ob_scr[pl.ds(r0, rb), :])
        g = g_ref[pl.ds(r0, rb), :].astype(_F32)
        o_ref[pl.ds(r0, rb), :] = (o * nw_ref[...] * (g * _sigmoid(g))).astype(o_ref.dtype)
        return carry

    lax.fori_loop(0, seq // rb, finish, 0)


def _hgrn(z, lb_f, lb_b, norm_w, batch, seq, col0, f_b, d_b, layer):
    n_heads = d_b // LANES
    h0 = col0 // LANES
    kh = f_b // LANES
    zspec = lambda off: pl.BlockSpec((seq, LANES), lambda b, h: (b, off + h))
    depth1 = lb_f.shape[0]
    return pl.pallas_call(
        functools.partial(_hgrn_body, layer=layer),
        grid=(batch, n_heads),
        in_specs=[zspec(h0), zspec(h0 + kh), zspec(h0 + 2 * kh), zspec(h0 + 3 * kh),
                  zspec(h0 + 3 * kh + n_heads),
                  pl.BlockSpec((depth1, LANES), lambda b, h: (0, h)),
                  pl.BlockSpec((depth1, LANES), lambda b, h: (0, h)),
                  pl.BlockSpec((1, LANES), lambda b, h: (0, 0))],
        out_specs=pl.BlockSpec((seq, LANES), lambda b, h: (b, h)),
        out_shape=jax.ShapeDtypeStruct((batch * seq, d_b), _BF16),
        scratch_shapes=[pltpu.VMEM((seq, LANES), _F32), pltpu.VMEM((seq, LANES), _F32),
                        pltpu.VMEM((LANES, LANES), _F32), pltpu.VMEM((LANES, LANES), _F32)],
        compiler_params=_params("parallel", "parallel"),
        name="hgrn2",
    )(z, z, z, z, z, lb_f, lb_b, norm_w)


def _outproj_body(a_ref, b_ref, wa_ref, wb_ref, x_ref, mg_ref, wr_ref, br_ref,
                  x1_ref, h2_ref, lg_ref):
    x1 = x_ref[...] + _dot(a_ref[...], wa_ref[...]) + _dot(b_ref[...], wb_ref[...])
    x1_ref[...] = x1
    h2 = _rms_scale(x1) * mg_ref[...]
    h2_ref[...] = h2
    lg_ref[...] = _dot(h2.astype(_BF16), wr_ref[...]) + br_ref[...]


def _out_proj(a, b, wa_bf, wb_bf, x2, moe_gain, wr_bf, b_router):
    n, d = x2.shape
    d_a, d_b, e = a.shape[1], b.shape[1], wr_bf.shape[1]
    tm = _tile(n, 512, 8)
    full = lambda shape: pl.BlockSpec(shape, lambda i: (0,) * len(shape))
    row = lambda w: pl.BlockSpec((tm, w), lambda i: (i, 0))
    return pl.pallas_call(
        _outproj_body,
        grid=(n // tm,),
        in_specs=[row(d_a), row(d_b), full((d_a, d)), full((d_b, d)), row(d),
                  full((1, d)), full((d, e)), full((1, e))],
        out_specs=[row(d), row(d), row(e)],
        out_shape=[jax.ShapeDtypeStruct((n, d), _F32), jax.ShapeDtypeStruct((n, d), _F32),
                   jax.ShapeDtypeStruct((n, e), _F32)],
        compiler_params=_params("parallel"),
        name="out_proj",
    )(a, b, wa_bf, wb_bf, x2, moe_gain, wr_bf, b_router)


def _route_body(lg_ref, idx_ref, gate_ref, rank_ref, cnt_ref, carry_scr):
    @pl.when(pl.program_id(0) == 0)
    def _():
        carry_scr[...] = jnp.zeros_like(carry_scr)

    l = lg_ref[...]
    tm, e = l.shape
    lane = lax.broadcasted_iota(jnp.int32, (tm, e), 1).astype(_F32)
    vals, idxs, hots = [], [], []
    for _ in range(TOP_K):
        m = jnp.max(l, axis=-1, keepdims=True)
        ix = jnp.min(jnp.where(l == m, lane, float(e)), axis=-1, keepdims=True)
        hot = lane == ix
        vals.append(m)
        idxs.append(ix)
        hots.append(hot)
        l = jnp.where(hot, -jnp.inf, l)
    exps = [jnp.exp(v - vals[0]) for v in vals]
    denom = functools.reduce(lambda p, q: p + q, exps)
    sel = functools.reduce(lambda p, q: p + q, [h.astype(_F32) for h in hots])
    ti = lax.broadcasted_iota(jnp.int32, (tm, tm), 0)
    si = lax.broadcasted_iota(jnp.int32, (tm, tm), 1)
    before = _dot((si < ti).astype(_BF16), sel.astype(_BF16)) + carry_scr[...]
    slot = lax.broadcasted_iota(jnp.int32, (tm, TOP_K), 1)
    idx_out = jnp.zeros((tm, TOP_K), _F32)
    gate_out = jnp.zeros((tm, TOP_K), _F32)
    rank_out = jnp.zeros((tm, TOP_K), _F32)
    for k in range(TOP_K):
        rank_k = jnp.sum(jnp.where(hots[k], before, 0.0), axis=-1, keepdims=True)
        idx_out = jnp.where(slot == k, idxs[k], idx_out)
        gate_out = jnp.where(slot == k, exps[k] / denom, gate_out)
        rank_out = jnp.where(slot == k, rank_k, rank_out)
    idx_ref[...] = idx_out.astype(jnp.int32)
    gate_ref[...] = gate_out
    rank_ref[...] = rank_out.astype(jnp.int32)
    carry_scr[...] += jnp.sum(sel, axis=0, keepdims=True)
    cnt_ref[...] = carry_scr[...].astype(jnp.int32)


def _route(logits):
    n, e = logits.shape
    tm = _tile(n, 512, 8)
    slot = pl.BlockSpec((tm, TOP_K), lambda i: (i, 0))
    return pl.pallas_call(
        _route_body,
        grid=(n // tm,),
        in_specs=[pl.BlockSpec((tm, e), lambda i: (i, 0))],
        out_specs=[slot, slot, slot, pl.BlockSpec((1, e), lambda i: (0, 0))],
        out_shape=[jax.ShapeDtypeStruct((n, TOP_K), jnp.int32), jax.ShapeDtypeStruct((n, TOP_K), _F32),
                   jax.ShapeDtypeStruct((n, TOP_K), jnp.int32), jax.ShapeDtypeStruct((1, e), jnp.int32)],
        scratch_shapes=[pltpu.VMEM((1, e), _F32)],
        compiler_params=_params("arbitrary"),
        name="route",
    )(logits)


SUBLANES = 8


def _pad_chunks(bm):
    chunks, s = [], bm // 2
    while s >= SUBLANES:
        chunks.append((s, None))
        s //= 2
    return chunks + [(1, r) for r in range(SUBLANES - 1)]


def _dispatch_body(dest_ref, padlo_ref, padn_ref, nv_ref, h2_ref, xs_hbm, zero_scr, sem, zsem, *,
                   bm, n_experts):
    tm = h2_ref.shape[0]

    def issue(t, carry):
        for k in range(TOP_K):
            pltpu.make_async_copy(h2_ref.at[pl.ds(t, 1), :],
                                  xs_hbm.at[pl.ds(dest_ref[t * TOP_K + k], 1), :], sem).start()
        return carry

    lax.fori_loop(0, tm, issue, 0)

    def pad_copy(e, size, single):
        lo, n_pad = padlo_ref[e], padn_ref[e]
        if single is not None:
            on = single < (n_pad & (SUBLANES - 1))
            start = lo + single
        else:
            on = (n_pad & size) != 0
            above = (n_pad // (2 * size)) * (2 * size)
            start = pl.multiple_of(lo + n_pad - above - size, SUBLANES)
        return on, pltpu.make_async_copy(zero_scr.at[pl.ds(0, size), :],
                                         xs_hbm.at[pl.ds(start, size), :], zsem)

    @pl.when(pl.program_id(0) == 0)
    def _():
        zero_scr[...] = jnp.zeros_like(zero_scr)

        def start_pads(e, carry):
            for size, single in _pad_chunks(bm):
                on, cp = pad_copy(e, size, single)
                pl.when(on)(cp.start)
            return carry

        def wait_pads(e, carry):
            for size, single in _pad_chunks(bm):
                on, cp = pad_copy(e, size, single)
                pl.when(on)(cp.wait)
            return carry

        def tail_copy(h):
            return pltpu.make_async_copy(
                zero_scr, xs_hbm.at[pl.ds(pl.multiple_of(h * (bm // 2), SUBLANES), bm // 2), :], zsem)

        n_halves = xs_hbm.shape[0] // (bm // 2)
        lax.fori_loop(0, n_experts, start_pads, 0)
        lax.fori_loop(2 * nv_ref[0], n_halves, lambda h, c: (tail_copy(h).start(), c)[1], 0)
        lax.fori_loop(0, n_experts, wait_pads, 0)
        lax.fori_loop(2 * nv_ref[0], n_halves, lambda h, c: (tail_copy(h).wait(), c)[1], 0)

    for k in range(TOP_K):
        pltpu.make_async_copy(h2_ref, xs_hbm.at[pl.ds(0, tm), :], sem).wait()


def _dispatch(h2p, dest_flat, pad_lo, pad_n, n_valid, n_rows, bm):
    n, c = h2p.shape
    tm = _tile(n, 256, 8)
    smem = lambda shape, imap: pl.BlockSpec(shape, imap, memory_space=pltpu.SMEM)
    n_experts = pad_lo.shape[0]
    return pl.pallas_call(
        functools.partial(_dispatch_body, bm=bm, n_experts=n_experts),
        grid=(n // tm,),
        in_specs=[smem((tm * TOP_K,), lambda i: (i,)),
                  smem((n_experts,), lambda i: (0,)),
                  smem((n_experts,), lambda i: (0,)),
                  smem((1,), lambda i: (0,)),
                  pl.BlockSpec((tm, c), lambda i: (i, 0))],
        out_specs=pl.BlockSpec(memory_space=pl.ANY),
        out_shape=jax.ShapeDtypeStruct((n_rows, c), h2p.dtype),
        scratch_shapes=[pltpu.VMEM((bm // 2, c), h2p.dtype),
                        pltpu.SemaphoreType.DMA(()), pltpu.SemaphoreType.DMA(())],
        compiler_params=_params("arbitrary", row_dma_loop=True),
        name="dispatch",
    )(dest_flat, pad_lo, pad_n, n_valid, h2p)


def _first_of_expert(be_ref, nv_ref, i):
    changed = jnp.logical_or(i == 0, be_ref[i] != be_ref[jnp.maximum(i - 1, 0)])
    return jnp.logical_and(i < nv_ref[0], changed)


def _moe_up_body(be_ref, nv_ref, xs_ref, w1_ref, b1_ref, o_ref, w1b_scr, xb_scr):
    i = pl.program_id(0)
    j = pl.program_id(1)
    bm, tn = o_ref.shape
    valid = i < nv_ref[0]

    @pl.when(_first_of_expert(be_ref, nv_ref, i))
    def _():
        w1b_scr[j] = w1_ref[0].astype(_BF16)

    @pl.when(jnp.logical_and(valid, j == 0))
    def _():
        xb_scr[...] = xs_ref[...].astype(_BF16)

    @pl.when(valid)
    def _():
        h = _dot(xb_scr[...], w1b_scr[j]) + b1_ref[0]
        even = (lax.broadcasted_iota(jnp.int32, (bm, LANES), 1) & 1) == 0
        for g in range(tn // LANES):
            pa = h[:, (2 * g) * LANES:(2 * g + 1) * LANES]
            pb = h[:, (2 * g + 1) * LANES:(2 * g + 2) * LANES]
            glu = jnp.where(even, pa, pltpu.roll(pb, 1, 1))
            lin = jnp.where(even, pltpu.roll(pa, LANES - 1, 1), pb)
            glu = jnp.minimum(glu, SWIGLU_LIMIT)
            lin = jnp.clip(lin, -SWIGLU_LIMIT, SWIGLU_LIMIT)
            o_ref[:, g * LANES:(g + 1) * LANES] = (
                glu * _sigmoid(SWIGLU_ALPHA * glu) * (lin + 1.0)).astype(o_ref.dtype)

    @pl.when(jnp.logical_not(valid))
    def _():
        o_ref[...] = jnp.zeros_like(o_ref)


def _moe_down_body(be_ref, nv_ref, act_ref, w2_ref, b2_ref, y_ref, w2b_scr, perm_scr):
    i = pl.program_id(0)
    j = pl.program_id(1)
    _, f, tn = w2b_scr.shape
    valid = i < nv_ref[0]
    half = LANES // 2

    @pl.when(_first_of_expert(be_ref, nv_ref, i))
    def _():
        for cb in range(tn // LANES):
            cols = slice(cb * LANES, (cb + 1) * LANES)
            for g in range(f // LANES):
                base = g * LANES
                perm_scr[pl.ds(base, half, stride=2), :] = w2_ref[0, pl.ds(base, half), cols]
                perm_scr[pl.ds(base + 1, half, stride=2), :] = w2_ref[0, pl.ds(base + half, half), cols]
            w2b_scr[j, :, cols] = perm_scr[...].astype(_BF16)

    @pl.when(valid)
    def _():
        y_ref[...] = _dot(act_ref[...], w2b_scr[j]) + b2_ref[0]

    @pl.when(jnp.logical_not(valid))
    def _():
        y_ref[...] = jnp.zeros_like(y_ref)


def _moe_block_rows(assignments, n_experts):
    bm = 128
    while bm < 512 and bm * 2 * 4 <= assignments // n_experts:
        bm *= 2
    return bm


def _moe(xs, block_e, n_valid, w1, b1, w2, b2, bm):
    p, d = xs.shape
    e, _, f2 = w1.shape
    f = f2 // 2
    n_blocks = p // bm
    tn_up = _tile(f, 512, LANES)
    tn_dn = _tile(d, 1024, LANES)
    j_up, j_dn = f // tn_up, d // tn_dn

    def blk(i, nv):
        return jnp.minimum(i, nv[0] - 1)

    def tile_idx(n_j):
        return lambda i, j, be, nv: (be[blk(i, nv)], 0, jnp.where(_first_of_expert(be, nv, i), j, n_j - 1))

    act = pl.pallas_call(
        _moe_up_body,
        grid_spec=pltpu.PrefetchScalarGridSpec(
            num_scalar_prefetch=2,
            grid=(n_blocks, j_up),
            in_specs=[
                pl.BlockSpec((bm, d), lambda i, j, be, nv: (blk(i, nv), 0)),
                pl.BlockSpec((1, d, 2 * tn_up), tile_idx(j_up)),
                pl.BlockSpec((1, 1, 2 * tn_up), lambda i, j, be, nv: (be[blk(i, nv)], 0, j)),
            ],
            out_specs=pl.BlockSpec((bm, tn_up), lambda i, j, be, nv: (i, j)),
            scratch_shapes=[pltpu.VMEM((j_up, d, 2 * tn_up), _BF16), pltpu.VMEM((bm, d), _BF16)],
        ),
        out_shape=jax.ShapeDtypeStruct((p, f), _BF16),
        compiler_params=_params("arbitrary", "arbitrary"),
        name="moe_up",
    )(block_e, n_valid, xs, w1, b1.reshape(e, 1, f2))

    return pl.pallas_call(
        _moe_down_body,
        grid_spec=pltpu.PrefetchScalarGridSpec(
            num_scalar_prefetch=2,
            grid=(n_blocks, j_dn),
            in_specs=[
                pl.BlockSpec((bm, f), lambda i, j, be, nv: (blk(i, nv), 0)),
                pl.BlockSpec((1, f, tn_dn), tile_idx(j_dn)),
                pl.BlockSpec((1, 1, tn_dn), lambda i, j, be, nv: (be[blk(i, nv)], 0, j)),
            ],
            out_specs=pl.BlockSpec((bm, tn_dn), lambda i, j, be, nv: (i, j)),
            scratch_shapes=[pltpu.VMEM((j_dn, f, tn_dn), _BF16), pltpu.VMEM((f, LANES), _F32)],
        ),
        out_shape=jax.ShapeDtypeStruct((p, d), _F32),
        compiler_params=_params("arbitrary", "arbitrary"),
        name="moe_down",
    )(block_e, n_valid, act, w2, b2.reshape(e, 1, d))


def _combine_body(dest_ref, gate_ref, x1_ref, fw_ref, y_hbm, o_ref, y_scr, sem):
    tm, d = x1_ref.shape

    def issue(t, carry):
        for k in range(TOP_K):
            pltpu.make_async_copy(y_hbm.at[pl.ds(dest_ref[t * TOP_K + k], 1), :],
                                  y_scr.at[pl.ds(k * tm + t, 1), :], sem).start()
        return carry

    lax.fori_loop(0, tm, issue, 0)
    pltpu.make_async_copy(y_hbm.at[pl.ds(0, TOP_K * tm), :], y_scr, sem).wait()
    acc = x1_ref[...]
    gates = gate_ref[...]
    for k in range(TOP_K):
        acc = acc + y_scr[pl.ds(k * tm, tm), :] * gates[:, k:k + 1]
    o_ref[...] = _rms_scale(acc) * fw_ref[...]


def _combine(dest_flat, gates, x1, final_gain, y):
    n, d = x1.shape
    tm = _tile(n, 256, 8)
    return pl.pallas_call(
        _combine_body,
        grid=(n // tm,),
        in_specs=[pl.BlockSpec((tm * TOP_K,), lambda i: (i,), memory_space=pltpu.SMEM),
                  pl.BlockSpec((tm, TOP_K), lambda i: (i, 0)),
                  pl.BlockSpec((tm, d), lambda i: (i, 0)),
                  pl.BlockSpec((1, d), lambda i: (0, 0)),
                  pl.BlockSpec(memory_space=pl.ANY)],
        out_specs=pl.BlockSpec((tm, d), lambda i: (i, 0)),
        out_shape=jax.ShapeDtypeStruct((n, d), _F32),
        scratch_shapes=[pltpu.VMEM((TOP_K * tm, d), _F32), pltpu.SemaphoreType.DMA(())],
        compiler_params=_params("arbitrary", row_dma_loop=True),
        name="combine",
    )(dest_flat, gates, x1, final_gain, y)


def kernel(x, mix_norm, w_in, ln_v_w, ln_v_b, w_spatial, b_spatial, gmlp_out_norm, hgrn_lb_fwd,
           hgrn_lb_bwd, hgrn_out_norm, w_out, moe_norm, w_router, b_router, w1, b1, w2, b2, final_norm):
    batch, seq, d = x.shape
    depth = mix_norm.shape[0]
    assert depth == 1, "single trunk layer"
    layer = 0
    d_a = ln_v_w.shape[1]
    d_b = w_out.shape[1] - d_a
    f_b = hgrn_lb_fwd.shape[1]
    n_experts = w_router.shape[2]
    n = batch * seq
    assert seq % GMLP_CHUNK == 0 and seq % SCAN_CHUNK == 0 and d_a % LANES == 0 and d_a == d_b == f_b
    assert hgrn_out_norm.shape[1] == LANES and w_spatial.shape[2] == GMLP_CHUNK

    x2 = x.reshape(n, d)
    z = _in_proj(x2, mix_norm[layer][None], w_in[layer].astype(_BF16))
    a = _gmlp(z, ln_v_w[layer][None], ln_v_b[layer][None], w_spatial[layer].astype(_BF16),
              b_spatial[layer][:, :, None], gmlp_out_norm[layer][None], d_a)
    bo = _hgrn(z, hgrn_lb_fwd, hgrn_lb_bwd, hgrn_out_norm[layer][None], batch, seq, 2 * d_a, f_b, d_b, layer)
    w_out_bf = w_out[layer].astype(_BF16)
    x1, h2p, logits = _out_proj(a, bo, w_out_bf[:d_a], w_out_bf[d_a:], x2, moe_norm[layer][None],
                                w_router[layer].astype(_BF16), b_router[layer][None])

    idx, gates, rank, counts = _route(logits)

    assignments = n * TOP_K
    bm = _moe_block_rows(assignments, n_experts)
    n_blocks = -(-assignments // bm) + n_experts
    counts = counts[0]
    padded = (counts + bm - 1) // bm * bm
    pad_end = jnp.cumsum(padded)
    start_pad = pad_end - padded
    dest = (start_pad[idx] + rank).astype(jnp.int32).reshape(-1)
    block_start = jnp.arange(n_blocks, dtype=jnp.int32) * bm
    block_e = jnp.minimum(jnp.sum(block_start[:, None] >= pad_end[None, :], axis=1),
                          n_experts - 1).astype(jnp.int32)
    n_valid = (pad_end[-1:] // bm).astype(jnp.int32)

    xs = _dispatch(h2p, dest, (start_pad + counts).astype(jnp.int32), (padded - counts).astype(jnp.int32),
                   n_valid, n_blocks * bm, bm)
    y = _moe(xs, block_e, n_valid, w1[layer], b1[layer], w2[layer], b2[layer], bm)
    out = _combine(dest, gates, x1, final_norm[None], y)
    return out.reshape(batch, seq, d)
```

```python
import functools

import jax
import jax.numpy as jnp
from jax import lax
from jax.experimental import pallas as pl
from jax.experimental.pallas import tpu as pltpu

EPS = 1e-5
LANES = 128
GMLP_CHUNK = 128
SCAN_CHUNK = 64
TOP_K = 4
SWIGLU_LIMIT = 7.0
SWIGLU_ALPHA = 1.702
VMEM_LIMIT = 56 * 1024 * 1024

_F32 = jnp.float32
_BF16 = jnp.bfloat16


def _tile(n, target, align):
    if n <= target:
        return n
    t = target - target % align
    while t > align and n % t:
        t -= align
    assert n % t == 0, (n, target, align)
    return t


def _params(*semantics, row_dma_loop=False):
    return pltpu.CompilerParams(dimension_semantics=semantics, vmem_limit_bytes=VMEM_LIMIT,
                                disable_bounds_checks=row_dma_loop)


def _dot(a, b):
    return jnp.dot(a, b, preferred_element_type=_F32)


def _dot_nt(a, b):
    return lax.dot_general(a, b, (((1,), (1,)), ((), ())), preferred_element_type=_F32)


def _dot_tn(a, b):
    return lax.dot_general(a, b, (((0,), (0,)), ((), ())), preferred_element_type=_F32)


def _rms_scale(x):
    return x * lax.rsqrt(jnp.mean(x * x, axis=-1, keepdims=True) + EPS)


def _sigmoid(x):
    return jax.nn.sigmoid(x)


def _inproj_body(x_ref, g_ref, w_ref, o_ref, h_scr):
    @pl.when(pl.program_id(1) == 0)
    def _():
        h_scr[...] = (_rms_scale(x_ref[...]) * g_ref[...]).astype(h_scr.dtype)

    o_ref[...] = _dot(h_scr[...], w_ref[...]).astype(o_ref.dtype)


def _in_proj(x2, gain, w_bf):
    n, d = x2.shape
    c = w_bf.shape[1]
    tm = _tile(n, 1024, 8)
    tn = _tile(c, 1024, LANES)
    return pl.pallas_call(
        _inproj_body,
        grid=(n // tm, c // tn),
        in_specs=[pl.BlockSpec((tm, d), lambda i, j: (i, 0)),
                  pl.BlockSpec((1, d), lambda i, j: (0, 0)),
                  pl.BlockSpec((d, tn), lambda i, j: (0, j))],
        out_specs=pl.BlockSpec((tm, tn), lambda i, j: (i, j)),
        out_shape=jax.ShapeDtypeStruct((n, c), _BF16),
        scratch_shapes=[pltpu.VMEM((tm, d), _BF16)],
        compiler_params=_params("parallel", "arbitrary"),
        name="in_proj",
    )(x2, gain, w_bf)


def _gelu(x):
    return 0.5 * x * (1.0 + lax.erf(x * (2.0 ** -0.5)))


def _gmlp_body(u_ref, v_ref, lnw_ref, lnb_ref, ws_ref, bs_ref, gw_ref, o_ref, a_scr, *, n_heads):
    c = GMLP_CHUNK
    d_a = n_heads * LANES

    def chunk(ci, carry):
        r0 = pl.multiple_of(ci * c, c)
        u = _gelu(u_ref[pl.ds(r0, c), :].astype(_F32))
        v = _gelu(v_ref[pl.ds(r0, c), :].astype(_F32))
        mu = jnp.mean(v, axis=-1, keepdims=True)
        vc = v - mu
        var = jnp.mean(vc * vc, axis=-1, keepdims=True)
        vn = ((vc * lax.rsqrt(var + EPS)) * lnw_ref[...] + lnb_ref[...]).astype(_BF16)
        ss = jnp.zeros((c, 1), _F32)
        for h in range(n_heads):
            cols = slice(h * LANES, (h + 1) * LANES)
            mixed = _dot(ws_ref[h], vn[:, cols]) + bs_ref[h]
            a = u[:, cols] * mixed
            ss = ss + jnp.sum(a * a, axis=-1, keepdims=True)
            a_scr[:, cols] = a
        scale = lax.rsqrt(ss * (1.0 / d_a) + EPS)
        o_ref[pl.ds(r0, c), :] = (a_scr[...] * scale * gw_ref[...]).astype(o_ref.dtype)
        return carry

    lax.fori_loop(0, u_ref.shape[0] // c, chunk, 0)


def _gmlp(z, ln_w, ln_b, ws_bf, bs_col, out_gain, d_a):
    n = z.shape[0]
    n_heads = d_a // LANES
    rows = _tile(n, 512, GMLP_CHUNK)
    full = lambda shape: pl.BlockSpec(shape, lambda i: (0,) * len(shape))
    return pl.pallas_call(
        functools.partial(_gmlp_body, n_heads=n_heads),
        grid=(n // rows,),
        in_specs=[pl.BlockSpec((rows, d_a), lambda i: (i, 0)),
                  pl.BlockSpec((rows, d_a), lambda i: (i, 1)),
                  full((1, d_a)), full((1, d_a)),
                  full((n_heads, GMLP_CHUNK, GMLP_CHUNK)),
                  full((n_heads, GMLP_CHUNK, 1)),
                  full((1, d_a))],
        out_specs=pl.BlockSpec((rows, d_a), lambda i: (i, 0)),
        out_shape=jax.ShapeDtypeStruct((n, d_a), _BF16),
        scratch_shapes=[pltpu.VMEM((GMLP_CHUNK, d_a), _F32)],
        compiler_params=_params("parallel"),
        name="gmlp",
    )(z, z, ln_w, ln_b, ws_bf, bs_col, out_gain)


def _hgrn_body(q_ref, ff_ref, fb_ref, i_ref, g_ref, lbf_ref, lbb_ref, nw_ref, o_ref,
               of_scr, ob_scr, sf_scr, sb_scr, *, layer):
    c = SCAN_CHUNK
    seq = q_ref.shape[0]
    n_chunks = seq // c

    def lower_bound(t_ref):
        t = t_ref[...].astype(_F32)
        e = jnp.exp(t - jnp.max(t, axis=0, keepdims=True))
        sm = e / jnp.sum(e, axis=0, keepdims=True)
        return jnp.sum(sm[:layer + 1], axis=0, keepdims=True)

    lb_f = lower_bound(lbf_ref)
    lb_b = lower_bound(lbb_ref)
    sf_scr[...] = jnp.zeros_like(sf_scr)
    sb_scr[...] = jnp.zeros_like(sb_scr)

    row = lax.broadcasted_iota(jnp.int32, (c, LANES), 0)
    ti = lax.broadcasted_iota(jnp.int32, (c, c), 0)
    si = lax.broadcasted_iota(jnp.int32, (c, c), 1)

    def scan(x, reverse):
        d = 1
        while d < c:
            if reverse:
                x = x + jnp.where(row < c - d, pltpu.roll(x, c - d, 0), 0.0)
            else:
                x = x + jnp.where(row >= d, pltpu.roll(x, d, 0), 0.0)
            d *= 2
        return x

    def one_dir(r0, f_ref, lb, s_scr, out_scr, reverse):
        q = q_ref[pl.ds(r0, c), :].astype(_F32)
        q = q * _sigmoid(q)
        f = lb + (1.0 - lb) * _sigmoid(f_ref[pl.ds(r0, c), :].astype(_F32))
        k = 1.0 - f
        cum = scan(jnp.log(f), reverse)
        tot = cum[0:1, :] if reverse else cum[c - 1:c, :]
        qd = (q * jnp.exp(cum)).astype(_BF16)
        kd = (k * jnp.exp(-cum)).astype(_BF16)
        kt = (k * jnp.exp(tot - cum)).astype(_BF16)
        v = i_ref[pl.ds(r0, c), :]
        mask = (si >= ti) if reverse else (si <= ti)
        sc = jnp.where(mask, _dot_nt(qd, kd), 0.0).astype(_BF16)
        s_prev = s_scr[...]
        out_scr[pl.ds(r0, c), :] = _dot(sc, v) + _dot_nt(qd, s_prev.astype(_BF16))
        s_scr[...] = s_prev * jnp.exp(tot) + _dot_tn(v, kt)

    def step(ci, carry):
        one_dir(pl.multiple_of(ci * c, c), ff_ref, lb_f, sf_scr, of_scr, False)
        one_dir(pl.multiple_of((n_chunks - 1 - ci) * c, c), fb_ref, lb_b, sb_scr, ob_scr, True)
        return carry

    lax.fori_loop(0, n_chunks, step, 0, unroll=2)

    rb = _tile(seq, 256, c)

    def finish(bi, carry):
        r0 = pl.multiple_of(bi * rb, rb)
        o = _rms_scale(of_scr[pl.ds(r0, rb), :] + ob_scr[pl.ds(r0, rb), :])
        g = g_ref[pl.ds(r0, rb), :].astype(_F32)
        o_ref[pl.ds(r0, rb), :] = (o * nw_ref[...] * (g * _sigmoid(g))).astype(o_ref.dtype)
        return carry

    lax.fori_loop(0, seq // rb, finish, 0)


def _hgrn(z, lb_f, lb_b, norm_w, batch, seq, col0, f_b, d_b, layer):
    n_heads = d_b // LANES
    h0 = col0 // LANES
    kh = f_b // LANES
    zspec = lambda off: pl.BlockSpec((seq, LANES), lambda b, h: (b, off + h))
    depth1 = lb_f.shape[0]
    return pl.pallas_call(
        functools.partial(_hgrn_body, layer=layer),
        grid=(batch, n_heads),
        in_specs=[zspec(h0), zspec(h0 + kh), zspec(h0 + 2 * kh), zspec(h0 + 3 * kh),
                  zspec(h0 + 3 * kh + n_heads),
                  pl.BlockSpec((depth1, LANES), lambda b, h: (0, h)),
                  pl.BlockSpec((depth1, LANES), lambda b, h: (0, h)),
                  pl.BlockSpec((1, LANES), lambda b, h: (0, 0))],
        out_specs=pl.BlockSpec((seq, LANES), lambda b, h: (b, h)),
        out_shape=jax.ShapeDtypeStruct((batch * seq, d_b), _BF16),
        scratch_shapes=[pltpu.VMEM((seq, LANES), _F32), pltpu.VMEM((seq, LANES), _F32),
                        pltpu.VMEM((LANES, LANES), _F32), pltpu.VMEM((LANES, LANES), _F32)],
        compiler_params=_params("parallel", "parallel"),
        name="hgrn2",
    )(z, z, z, z, z, lb_f, lb_b, norm_w)


def _outproj_body(a_ref, b_ref, wa_ref, wb_ref, x_ref, mg_ref, wr_ref, br_ref,
                  x1_ref, h2_ref, lg_ref):
    x1 = x_ref[...] + _dot(a_ref[...], wa_ref[...]) + _dot(b_ref[...], wb_ref[...])
    x1_ref[...] = x1
    h2 = _rms_scale(x1) * mg_ref[...]
    h2_ref[...] = h2
    lg_ref[...] = _dot(h2.astype(_BF16), wr_ref[...]) + br_ref[...]


def _out_proj(a, b, wa_bf, wb_bf, x2, moe_gain, wr_bf, b_router):
    n, d = x2.shape
    d_a, d_b, e = a.shape[1], b.shape[1], wr_bf.shape[1]
    tm = _tile(n, 512, 8)
    full = lambda shape: pl.BlockSpec(shape, lambda i: (0,) * len(shape))
    row = lambda w: pl.BlockSpec((tm, w), lambda i: (i, 0))
    return pl.pallas_call(
        _outproj_body,
        grid=(n // tm,),
        in_specs=[row(d_a), row(d_b), full((d_a, d)), full((d_b, d)), row(d),
                  full((1, d)), full((d, e)), full((1, e))],
        out_specs=[row(d), row(d), row(e)],
        out_shape=[jax.ShapeDtypeStruct((n, d), _F32), jax.ShapeDtypeStruct((n, d), _F32),
                   jax.ShapeDtypeStruct((n, e), _F32)],
        compiler_params=_params("parallel"),
        name="out_proj",
    )(a, b, wa_bf, wb_bf, x2, moe_gain, wr_bf, b_router)


def _route_body(lg_ref, idx_ref, gate_ref, rank_ref, cnt_ref, carry_scr):
    @pl.when(pl.program_id(0) == 0)
    def _():
        carry_scr[...] = jnp.zeros_like(carry_scr)

    l = lg_ref[...]
    tm, e = l.shape
    lane = lax.broadcasted_iota(jnp.int32, (tm, e), 1).astype(_F32)
    vals, idxs, hots = [], [], []
    for _ in range(TOP_K):
        m = jnp.max(l, axis=-1, keepdims=True)
        ix = jnp.min(jnp.where(l == m, lane, float(e)), axis=-1, keepdims=True)
        hot = lane == ix
        vals.append(m)
        idxs.append(ix)
        hots.append(hot)
        l = jnp.where(hot, -jnp.inf, l)
    exps = [jnp.exp(v - vals[0]) for v in vals]
    denom = functools.reduce(lambda p, q: p + q, exps)
    sel = functools.reduce(lambda p, q: p + q, [h.astype(_F32) for h in hots])
    ti = lax.broadcasted_iota(jnp.int32, (tm, tm), 0)
    si = lax.broadcasted_iota(jnp.int32, (tm, tm), 1)
    before = _dot((si < ti).astype(_BF16), sel.astype(_BF16)) + carry_scr[...]
    slot = lax.broadcasted_iota(jnp.int32, (tm, TOP_K), 1)
    idx_out = jnp.zeros((tm, TOP_K), _F32)
    gate_out = jnp.zeros((tm, TOP_K), _F32)
    rank_out = jnp.zeros((tm, TOP_K), _F32)
    for k in range(TOP_K):
        rank_k = jnp.sum(jnp.where(hots[k], before, 0.0), axis=-1, keepdims=True)
        idx_out = jnp.where(slot == k, idxs[k], idx_out)
        gate_out = jnp.where(slot == k, exps[k] / denom, gate_out)
        rank_out = jnp.where(slot == k, rank_k, rank_out)
    idx_ref[...] = idx_out.astype(jnp.int32)
    gate_ref[...] = gate_out
    rank_ref[...] = rank_out.astype(jnp.int32)
    carry_scr[...] += jnp.sum(sel, axis=0, keepdims=True)
    cnt_ref[...] = carry_scr[...].astype(jnp.int32)


def _route(logits):
    n, e = logits.shape
    tm = _tile(n, 512, 8)
    slot = pl.BlockSpec((tm, TOP_K), lambda i: (i, 0))
    return pl.pallas_call(
        _route_body,
        grid=(n // tm,),
        in_specs=[pl.BlockSpec((tm, e), lambda i: (i, 0))],
        out_specs=[slot, slot, slot, pl.BlockSpec((1, e), lambda i: (0, 0))],
        out_shape=[jax.ShapeDtypeStruct((n, TOP_K), jnp.int32), jax.ShapeDtypeStruct((n, TOP_K), _F32),
                   jax.ShapeDtypeStruct((n, TOP_K), jnp.int32), jax.ShapeDtypeStruct((1, e), jnp.int32)],
        scratch_shapes=[pltpu.VMEM((1, e), _F32)],
        compiler_params=_params("arbitrary"),
        name="route",
    )(logits)


SUBLANES = 8


def _pad_chunks(bm):
    chunks, s = [], bm // 2
    while s >= SUBLANES:
        chunks.append((s, None))
        s //= 2
    return chunks + [(1, r) for r in range(SUBLANES - 1)]


def _dispatch_body(dest_ref, padlo_ref, padn_ref, nv_ref, h2_ref, xs_hbm, zero_scr, sem, zsem, *,
                   bm, n_experts):
    tm = h2_ref.shape[0]

    def issue(t, carry):
        for k in range(TOP_K):
            pltpu.make_async_copy(h2_ref.at[pl.ds(t, 1), :],
                                  xs_hbm.at[pl.ds(dest_ref[t * TOP_K + k], 1), :], sem).start()
        return carry

    lax.fori_loop(0, tm, issue, 0)

    def pad_copy(e, size, single):
        lo, n_pad = padlo_ref[e], padn_ref[e]
        if single is not None:
            on = single < (n_pad & (SUBLANES - 1))
            start = lo + single
        else:
            on = (n_pad & size) != 0
            above = (n_pad // (2 * size)) * (2 * size)
            start = pl.multiple_of(lo + n_pad - above - size, SUBLANES)
        return on, pltpu.make_async_copy(zero_scr.at[pl.ds(0, size), :],
                                         xs_hbm.at[pl.ds(start, size), :], zsem)

    @pl.when(pl.program_id(0) == 0)
    def _():
        zero_scr[...] = jnp.zeros_like(zero_scr)

        def start_pads(e, carry):
            for size, single in _pad_chunks(bm):
                on, cp = pad_copy(e, size, single)
                pl.when(on)(cp.start)
            return carry

        def wait_pads(e, carry):
            for size, single in _pad_chunks(bm):
                on, cp = pad_copy(e, size, single)
                pl.when(on)(cp.wait)
            return carry

        def tail_copy(h):
            return pltpu.make_async_copy(
                zero_scr, xs_hbm.at[pl.ds(pl.multiple_of(h * (bm // 2), SUBLANES), bm // 2), :], zsem)

        n_halves = xs_hbm.shape[0] // (bm // 2)
        lax.fori_loop(0, n_experts, start_pads, 0)
        lax.fori_loop(2 * nv_ref[0], n_halves, lambda h, c: (tail_copy(h).start(), c)[1], 0)
        lax.fori_loop(0, n_experts, wait_pads, 0)
        lax.fori_loop(2 * nv_ref[0], n_halves, lambda h, c: (tail_copy(h).wait(), c)[1], 0)

    for k in range(TOP_K):
        pltpu.make_async_copy(h2_ref, xs_hbm.at[pl.ds(0, tm), :], sem).wait()


def _dispatch(h2p, dest_flat, pad_lo, pad_n, n_valid, n_rows, bm):
    n, c = h2p.shape
    tm = _tile(n, 256, 8)
    smem = lambda shape, imap: pl.BlockSpec(shape, imap, memory_space=pltpu.SMEM)
    n_experts = pad_lo.shape[0]
    return pl.pallas_call(
        functools.partial(_dispatch_body, bm=bm, n_experts=n_experts),
        grid=(n // tm,),
        in_specs=[smem((tm * TOP_K,), lambda i: (i,)),
                  smem((n_experts,), lambda i: (0,)),
                  smem((n_experts,), lambda i: (0,)),
                  smem((1,), lambda i: (0,)),
                  pl.BlockSpec((tm, c), lambda i: (i, 0))],
        out_specs=pl.BlockSpec(memory_space=pl.ANY),
        out_shape=jax.ShapeDtypeStruct((n_rows, c), h2p.dtype),
        scratch_shapes=[pltpu.VMEM((bm // 2, c), h2p.dtype),
                        pltpu.SemaphoreType.DMA(()), pltpu.SemaphoreType.DMA(())],
        compiler_params=_params("arbitrary", row_dma_loop=True),
        name="dispatch",
    )(dest_flat, pad_lo, pad_n, n_valid, h2p)


def _first_of_expert(be_ref, nv_ref, i):
    changed = jnp.logical_or(i == 0, be_ref[i] != be_ref[jnp.maximum(i - 1, 0)])
    return jnp.logical_and(i < nv_ref[0], changed)


def _moe_up_body(be_ref, nv_ref, xs_ref, w1_ref, b1_ref, o_ref, w1b_scr, xb_scr):
    i = pl.program_id(0)
    j = pl.program_id(1)
    bm, tn = o_ref.shape
    valid = i < nv_ref[0]

    @pl.when(_first_of_expert(be_ref, nv_ref, i))
    def _():
        w1b_scr[j] = w1_ref[0].astype(_BF16)

    @pl.when(jnp.logical_and(valid, j == 0))
    def _():
        xb_scr[...] = xs_ref[...].astype(_BF16)

    @pl.when(valid)
    def _():
        h = _dot(xb_scr[...], w1b_scr[j]) + b1_ref[0]
        even = (lax.broadcasted_iota(jnp.int32, (bm, LANES), 1) & 1) == 0
        for g in range(tn // LANES):
            pa = h[:, (2 * g) * LANES:(2 * g + 1) * LANES]
            pb = h[:, (2 * g + 1) * LANES:(2 * g + 2) * LANES]
            glu = jnp.where(even, pa, pltpu.roll(pb, 1, 1))
            lin = jnp.where(even, pltpu.roll(pa, LANES - 1, 1), pb)
            glu = jnp.minimum(glu, SWIGLU_LIMIT)
            lin = jnp.clip(lin, -SWIGLU_LIMIT, SWIGLU_LIMIT)
            o_ref[:, g * LANES:(g + 1) * LANES] = (
                glu * _sigmoid(SWIGLU_ALPHA * glu) * (lin + 1.0)).astype(o_ref.dtype)

    @pl.when(jnp.logical_not(valid))
    def _():
        o_ref[...] = jnp.zeros_like(o_ref)


def _moe_down_body(be_ref, nv_ref, act_ref, w2_ref, b2_ref, y_ref, w2b_scr, perm_scr):
    i = pl.program_id(0)
    j = pl.program_id(1)
    _, f, tn = w2b_scr.shape
    valid = i < nv_ref[0]
    half = LANES // 2

    @pl.when(_first_of_expert(be_ref, nv_ref, i))
    def _():
        for cb in range(tn // LANES):
            cols = slice(cb * LANES, (cb + 1) * LANES)
            for g in range(f // LANES):
                base = g * LANES
                perm_scr[pl.ds(base, half, stride=2), :] = w2_ref[0, pl.ds(base, half), cols]
                perm_scr[pl.ds(base + 1, half, stride=2), :] = w2_ref[0, pl.ds(base + half, half), cols]
            w2b_scr[j, :, cols] = perm_scr[...].astype(_BF16)

    @pl.when(valid)
    def _():
        y_ref[...] = _dot(act_ref[...], w2b_scr[j]) + b2_ref[0]

    @pl.when(jnp.logical_not(valid))
    def _():
        y_ref[...] = jnp.zeros_like(y_ref)


def _moe_block_rows(assignments, n_experts):
    bm = 128
    while bm < 1024 and bm * 2 * 2 <= assignments // n_experts:
        bm *= 2
    return bm


def _moe(xs, block_e, n_valid, w1, b1, w2, b2, bm):
    p, d = xs.shape
    e, _, f2 = w1.shape
    f = f2 // 2
    n_blocks = p // bm
    tn_up = _tile(f, 256, LANES)
    tn_dn = _tile(d, 1024, LANES)
    j_up, j_dn = f // tn_up, d // tn_dn

    def blk(i, nv):
        return jnp.minimum(i, nv[0] - 1)

    def tile_idx(n_j):
        return lambda i, j, be, nv: (be[blk(i, nv)], 0, jnp.where(_first_of_expert(be, nv, i), j, n_j - 1))

    act = pl.pallas_call(
        _moe_up_body,
        grid_spec=pltpu.PrefetchScalarGridSpec(
            num_scalar_prefetch=2,
            grid=(n_blocks, j_up),
            in_specs=[
                pl.BlockSpec((bm, d), lambda i, j, be, nv: (blk(i, nv), 0)),
                pl.BlockSpec((1, d, 2 * tn_up), tile_idx(j_up)),
                pl.BlockSpec((1, 1, 2 * tn_up), lambda i, j, be, nv: (be[blk(i, nv)], 0, j)),
            ],
            out_specs=pl.BlockSpec((bm, tn_up), lambda i, j, be, nv: (i, j)),
            scratch_shapes=[pltpu.VMEM((j_up, d, 2 * tn_up), _BF16), pltpu.VMEM((bm, d), _BF16)],
        ),
        out_shape=jax.ShapeDtypeStruct((p, f), _BF16),
        compiler_params=_params("arbitrary", "arbitrary"),
        name="moe_up",
    )(block_e, n_valid, xs, w1, b1.reshape(e, 1, f2))

    return pl.pallas_call(
        _moe_down_body,
        grid_spec=pltpu.PrefetchScalarGridSpec(
            num_scalar_prefetch=2,
            grid=(n_blocks, j_dn),
            in_specs=[
                pl.BlockSpec((bm, f), lambda i, j, be, nv: (blk(i, nv), 0)),
                pl.BlockSpec((1, f, tn_dn), tile_idx(j_dn)),
                pl.BlockSpec((1, 1, tn_dn), lambda i, j, be, nv: (be[blk(i, nv)], 0, j)),
            ],
            out_specs=pl.BlockSpec((bm, tn_dn), lambda i, j, be, nv: (i, j)),
            scratch_shapes=[pltpu.VMEM((j_dn, f, tn_dn), _BF16), pltpu.VMEM((f, LANES), _F32)],
        ),
        out_shape=jax.ShapeDtypeStruct((p, d), _F32),
        compiler_params=_params("arbitrary", "arbitrary"),
        name="moe_down",
    )(block_e, n_valid, act, w2, b2.reshape(e, 1, d))


def _combine_body(dest_ref, gate_ref, x1_ref, fw_ref, y_hbm, o_ref, y_scr, sem):
    tm, d = x1_ref.shape

    def issue(t, carry):
        for k in range(TOP_K):
            pltpu.make_async_copy(y_hbm.at[pl.ds(dest_ref[t * TOP_K + k], 1), :],
                                  y_scr.at[pl.ds(k * tm + t, 1), :], sem).start()
        return carry

    lax.fori_loop(0, tm, issue, 0)
    pltpu.make_async_copy(y_hbm.at[pl.ds(0, TOP_K * tm), :], y_scr, sem).wait()
    acc = x1_ref[...]
    gates = gate_ref[...]
    for k in range(TOP_K):
        acc = acc + y_scr[pl.ds(k * tm, tm), :] * gates[:, k:k + 1]
    o_ref[...] = _rms_scale(acc) * fw_ref[...]


def _combine(dest_flat, gates, x1, final_gain, y):
    n, d = x1.shape
    tm = _tile(n, 256, 8)
    return pl.pallas_call(
        _combine_body,
        grid=(n // tm,),
        in_specs=[pl.BlockSpec((tm * TOP_K,), lambda i: (i,), memory_space=pltpu.SMEM),
                  pl.BlockSpec((tm, TOP_K), lambda i: (i, 0)),
                  pl.BlockSpec((tm, d), lambda i: (i, 0)),
                  pl.BlockSpec((1, d), lambda i: (0, 0)),
                  pl.BlockSpec(memory_space=pl.ANY)],
        out_specs=pl.BlockSpec((tm, d), lambda i: (i, 0)),
        out_shape=jax.ShapeDtypeStruct((n, d), _F32),
        scratch_shapes=[pltpu.VMEM((TOP_K * tm, d), _F32), pltpu.SemaphoreType.DMA(())],
        compiler_params=_params("arbitrary", row_dma_loop=True),
        name="combine",
    )(dest_flat, gates, x1, final_gain, y)


def kernel(x, mix_norm, w_in, ln_v_w, ln_v_b, w_spatial, b_spatial, gmlp_out_norm, hgrn_lb_fwd,
           hgrn_lb_bwd, hgrn_out_norm, w_out, moe_norm, w_router, b_router, w1, b1, w2, b2, final_norm):
    batch, seq, d = x.shape
    depth = mix_norm.shape[0]
    assert depth == 1, "single trunk layer"
    layer = 0
    d_a = ln_v_w.shape[1]
    d_b = w_out.shape[1] - d_a
    f_b = hgrn_lb_fwd.shape[1]
    n_experts = w_router.shape[2]
    n = batch * seq
    assert seq % GMLP_CHUNK == 0 and seq % SCAN_CHUNK == 0 and d_a % LANES == 0 and d_a == d_b == f_b
    assert hgrn_out_norm.shape[1] == LANES and w_spatial.shape[2] == GMLP_CHUNK

    x2 = x.reshape(n, d)
    z = _in_proj(x2, mix_norm[layer][None], w_in[layer].astype(_BF16))
    a = _gmlp(z, ln_v_w[layer][None], ln_v_b[layer][None], w_spatial[layer].astype(_BF16),
              b_spatial[layer][:, :, None], gmlp_out_norm[layer][None], d_a)
    bo = _hgrn(z, hgrn_lb_fwd, hgrn_lb_bwd, hgrn_out_norm[layer][None], batch, seq, 2 * d_a, f_b, d_b, layer)
    w_out_bf = w_out[layer].astype(_BF16)
    x1, h2p, logits = _out_proj(a, bo, w_out_bf[:d_a], w_out_bf[d_a:], x2, moe_norm[layer][None],
                                w_router[layer].astype(_BF16), b_router[layer][None])

    idx, gates, rank, counts = _route(logits)

    assignments = n * TOP_K
    bm = _moe_block_rows(assignments, n_experts)
    n_blocks = -(-assignments // bm) + n_experts
    counts = counts[0]
    padded = (counts + bm - 1) // bm * bm
    pad_end = jnp.cumsum(padded)
    start_pad = pad_end - padded
    dest = (start_pad[idx] + rank).astype(jnp.int32).reshape(-1)
    block_start = jnp.arange(n_blocks, dtype=jnp.int32) * bm
    block_e = jnp.minimum(jnp.sum(block_start[:, None] >= pad_end[None, :], axis=1),
                          n_experts - 1).astype(jnp.int32)
    n_valid = (pad_end[-1:] // bm).astype(jnp.int32)

    xs = _dispatch(h2p, dest, (start_pad + counts).astype(jnp.int32), (padded - counts).astype(jnp.int32),
                   n_valid, n_blocks * bm, bm)
    y = _moe(xs, block_e, n_valid, w1[layer], b1[layer], w2[layer], b2[layer], bm)
    out = _combine(dest, gates, x1, final_norm[None], y)
    return out.reshape(batch, seq, d)
```

```python
import functools

import jax
import jax.numpy as jnp
from jax import lax
from jax.experimental import pallas as pl
from jax.experimental.pallas import tpu as pltpu

EPS = 1e-5
LANES = 128
GMLP_CHUNK = 128
SCAN_CHUNK = 64
TOP_K = 4
SWIGLU_LIMIT = 7.0
SWIGLU_ALPHA = 1.702
VMEM_LIMIT = 56 * 1024 * 1024

_F32 = jnp.float32
_BF16 = jnp.bfloat16


def _tile(n, target, align):
    if n <= target:
        return n
    t = target - target % align
    while t > align and n % t:
        t -= align
    assert n % t == 0, (n, target, align)
    return t


def _params(*semantics, row_dma_loop=False):
    return pltpu.CompilerParams(dimension_semantics=semantics, vmem_limit_bytes=VMEM_LIMIT,
                                disable_bounds_checks=row_dma_loop)


def _dot(a, b):
    return jnp.dot(a, b, preferred_element_type=_F32)


def _dot_nt(a, b):
    return lax.dot_general(a, b, (((1,), (1,)), ((), ())), preferred_element_type=_F32)


def _dot_tn(a, b):
    return lax.dot_general(a, b, (((0,), (0,)), ((), ())), preferred_element_type=_F32)


def _rms_scale(x):
    return x * lax.rsqrt(jnp.mean(x * x, axis=-1, keepdims=True) + EPS)


def _sigmoid(x):
    return jax.nn.sigmoid(x)


def _inproj_body(x_ref, g_ref, w_ref, o_ref, h_scr):
    @pl.when(pl.program_id(1) == 0)
    def _():
        h_scr[...] = (_rms_scale(x_ref[...]) * g_ref[...]).astype(h_scr.dtype)

    o_ref[...] = _dot(h_scr[...], w_ref[...]).astype(o_ref.dtype)


def _in_proj(x2, gain, w_bf):
    n, d = x2.shape
    c = w_bf.shape[1]
    tm = _tile(n, 1024, 8)
    tn = _tile(c, 1024, LANES)
    return pl.pallas_call(
        _inproj_body,
        grid=(n // tm, c // tn),
        in_specs=[pl.BlockSpec((tm, d), lambda i, j: (i, 0)),
                  pl.BlockSpec((1, d), lambda i, j: (0, 0)),
                  pl.BlockSpec((d, tn), lambda i, j: (0, j))],
        out_specs=pl.BlockSpec((tm, tn), lambda i, j: (i, j)),
        out_shape=jax.ShapeDtypeStruct((n, c), _BF16),
        scratch_shapes=[pltpu.VMEM((tm, d), _BF16)],
        compiler_params=_params("parallel", "arbitrary"),
        name="in_proj",
    )(x2, gain, w_bf)


def _gelu(x):
    return 0.5 * x * (1.0 + lax.erf(x * (2.0 ** -0.5)))


def _gmlp_body(u_ref, v_ref, lnw_ref, lnb_ref, ws_ref, bs_ref, gw_ref, o_ref, a_scr, *, n_heads):
    c = GMLP_CHUNK
    d_a = n_heads * LANES

    def chunk(ci, carry):
        r0 = pl.multiple_of(ci * c, c)
        u = _gelu(u_ref[pl.ds(r0, c), :].astype(_F32))
        v = _gelu(v_ref[pl.ds(r0, c), :].astype(_F32))
        mu = jnp.mean(v, axis=-1, keepdims=True)
        vc = v - mu
        var = jnp.mean(vc * vc, axis=-1, keepdims=True)
        vn = ((vc * lax.rsqrt(var + EPS)) * lnw_ref[...] + lnb_ref[...]).astype(_BF16)
        ss = jnp.zeros((c, 1), _F32)
        for h in range(n_heads):
            cols = slice(h * LANES, (h + 1) * LANES)
            mixed = _dot(ws_ref[h], vn[:, cols]) + bs_ref[h]
            a = u[:, cols] * mixed
            ss = ss + jnp.sum(a * a, axis=-1, keepdims=True)
            a_scr[:, cols] = a
        scale = lax.rsqrt(ss * (1.0 / d_a) + EPS)
        o_ref[pl.ds(r0, c), :] = (a_scr[...] * scale * gw_ref[...]).astype(o_ref.dtype)
        return carry

    lax.fori_loop(0, u_ref.shape[0] // c, chunk, 0)


def _gmlp(z, ln_w, ln_b, ws_bf, bs_col, out_gain, d_a):
    n = z.shape[0]
    n_heads = d_a // LANES
    rows = _tile(n, 512, GMLP_CHUNK)
    full = lambda shape: pl.BlockSpec(shape, lambda i: (0,) * len(shape))
    return pl.pallas_call(
        functools.partial(_gmlp_body, n_heads=n_heads),
        grid=(n // rows,),
        in_specs=[pl.BlockSpec((rows, d_a), lambda i: (i, 0)),
                  pl.BlockSpec((rows, d_a), lambda i: (i, 1)),
                  full((1, d_a)), full((1, d_a)),
                  full((n_heads, GMLP_CHUNK, GMLP_CHUNK)),
                  full((n_heads, GMLP_CHUNK, 1)),
                  full((1, d_a))],
        out_specs=pl.BlockSpec((rows, d_a), lambda i: (i, 0)),
        out_shape=jax.ShapeDtypeStruct((n, d_a), _BF16),
        scratch_shapes=[pltpu.VMEM((GMLP_CHUNK, d_a), _F32)],
        compiler_params=_params("parallel"),
        name="gmlp",
    )(z, z, ln_w, ln_b, ws_bf, bs_col, out_gain)


def _hgrn_body(q_ref, ff_ref, fb_ref, i_ref, g_ref, lbf_ref, lbb_ref, nw_ref, o_ref,
               of_scr, ob_scr, sf_scr, sb_scr, *, layer):
    c = SCAN_CHUNK
    seq = q_ref.shape[0]
    n_chunks = seq // c

    def lower_bound(t_ref):
        t = t_ref[...].astype(_F32)
        e = jnp.exp(t - jnp.max(t, axis=0, keepdims=True))
        sm = e / jnp.sum(e, axis=0, keepdims=True)
        return jnp.sum(sm[:layer + 1], axis=0, keepdims=True)

    lb_f = lower_bound(lbf_ref)
    lb_b = lower_bound(lbb_ref)
    sf_scr[...] = jnp.zeros_like(sf_scr)
    sb_scr[...] = jnp.zeros_like(sb_scr)

    row = lax.broadcasted_iota(jnp.int32, (c, LANES), 0)
    ti = lax.broadcasted_iota(jnp.int32, (c, c), 0)
    si = lax.broadcasted_iota(jnp.int32, (c, c), 1)

    def scan(x, reverse):
        d = 1
        while d < c:
            if reverse:
                x = x + jnp.where(row < c - d, pltpu.roll(x, c - d, 0), 0.0)
            else:
                x = x + jnp.where(row >= d, pltpu.roll(x, d, 0), 0.0)
            d *= 2
        return x

    def one_dir(r0, f_ref, lb, s_scr, out_scr, reverse):
        q = q_ref[pl.ds(r0, c), :].astype(_F32)
        q = q * _sigmoid(q)
        f = lb + (1.0 - lb) * _sigmoid(f_ref[pl.ds(r0, c), :].astype(_F32))
        k = 1.0 - f
        cum = scan(jnp.log(f), reverse)
        tot = cum[0:1, :] if reverse else cum[c - 1:c, :]
        qd = (q * jnp.exp(cum)).astype(_BF16)
        kd = (k * jnp.exp(-cum)).astype(_BF16)
        kt = (k * jnp.exp(tot - cum)).astype(_BF16)
        v = i_ref[pl.ds(r0, c), :]
        mask = (si >= ti) if reverse else (si <= ti)
        sc = jnp.where(mask, _dot_nt(qd, kd), 0.0).astype(_BF16)
        s_prev = s_scr[...]
        out_scr[pl.ds(r0, c), :] = _dot(sc, v) + _dot_nt(qd, s_prev.astype(_BF16))
        s_scr[...] = s_prev * jnp.exp(tot) + _dot_tn(v, kt)

    def step(ci, carry):
        one_dir(pl.multiple_of(ci * c, c), ff_ref, lb_f, sf_scr, of_scr, False)
        one_dir(pl.multiple_of((n_chunks - 1 - ci) * c, c), fb_ref, lb_b, sb_scr, ob_scr, True)
        return carry

    lax.fori_loop(0, n_chunks, step, 0, unroll=4)

    rb = _tile(seq, 256, c)

    def finish(bi, carry):
        r0 = pl.multiple_of(bi * rb, rb)
        o = _rms_scale(of_scr[pl.ds(r0, rb), :] + ob_scr[pl.ds(r0, rb), :])
        g = g_ref[pl.ds(r0, rb), :].astype(_F32)
        o_ref[pl.ds(r0, rb), :] = (o * nw_ref[...] * (g * _sigmoid(g))).astype(o_ref.dtype)
        return carry

    lax.fori_loop(0, seq // rb, finish, 0)


def _hgrn(z, lb_f, lb_b, norm_w, batch, seq, col0, f_b, d_b, layer):
    n_heads = d_b // LANES
    h0 = col0 // LANES
    kh = f_b // LANES
    zspec = lambda off: pl.BlockSpec((seq, LANES), lambda b, h: (b, off + h))
    depth1 = lb_f.shape[0]
    return pl.pallas_call(
        functools.partial(_hgrn_body, layer=layer),
        grid=(batch, n_heads),
        in_specs=[zspec(h0), zspec(h0 + kh), zspec(h0 + 2 * kh), zspec(h0 + 3 * kh),
                  zspec(h0 + 3 * kh + n_heads),
                  pl.BlockSpec((depth1, LANES), lambda b, h: (0, h)),
                  pl.BlockSpec((depth1, LANES), lambda b, h: (0, h)),
                  pl.BlockSpec((1, LANES), lambda b, h: (0, 0))],
        out_specs=pl.BlockSpec((seq, LANES), lambda b, h: (b, h)),
        out_shape=jax.ShapeDtypeStruct((batch * seq, d_b), _BF16),
        scratch_shapes=[pltpu.VMEM((seq, LANES), _F32), pltpu.VMEM((seq, LANES), _F32),
                        pltpu.VMEM((LANES, LANES), _F32), pltpu.VMEM((LANES, LANES), _F32)],
        compiler_params=_params("parallel", "parallel"),
        name="hgrn2",
    )(z, z, z, z, z, lb_f, lb_b, norm_w)


def _outproj_body(a_ref, b_ref, wa_ref, wb_ref, x_ref, mg_ref, wr_ref, br_ref,
                  x1_ref, h2_ref, lg_ref):
    x1 = x_ref[...] + _dot(a_ref[...], wa_ref[...]) + _dot(b_ref[...], wb_ref[...])
    x1_ref[...] = x1
    h2 = _rms_scale(x1) * mg_ref[...]
    h2_ref[...] = h2
    lg_ref[...] = _dot(h2.astype(_BF16), wr_ref[...]) + br_ref[...]


def _out_proj(a, b, wa_bf, wb_bf, x2, moe_gain, wr_bf, b_router):
    n, d = x2.shape
    d_a, d_b, e = a.shape[1], b.shape[1], wr_bf.shape[1]
    tm = _tile(n, 512, 8)
    full = lambda shape: pl.BlockSpec(shape, lambda i: (0,) * len(shape))
    row = lambda w: pl.BlockSpec((tm, w), lambda i: (i, 0))
    return pl.pallas_call(
        _outproj_body,
        grid=(n // tm,),
        in_specs=[row(d_a), row(d_b), full((d_a, d)), full((d_b, d)), row(d),
                  full((1, d)), full((d, e)), full((1, e))],
        out_specs=[row(d), row(d), row(e)],
        out_shape=[jax.ShapeDtypeStruct((n, d), _F32), jax.ShapeDtypeStruct((n, d), _F32),
                   jax.ShapeDtypeStruct((n, e), _F32)],
        compiler_params=_params("parallel"),
        name="out_proj",
    )(a, b, wa_bf, wb_bf, x2, moe_gain, wr_bf, b_router)


def _route_body(lg_ref, idx_ref, gate_ref, rank_ref, cnt_ref, carry_scr):
    @pl.when(pl.program_id(0) == 0)
    def _():
        carry_scr[...] = jnp.zeros_like(carry_scr)

    l = lg_ref[...]
    tm, e = l.shape
    lane = lax.broadcasted_iota(jnp.int32, (tm, e), 1).astype(_F32)
    vals, idxs, hots = [], [], []
    for _ in range(TOP_K):
        m = jnp.max(l, axis=-1, keepdims=True)
        ix = jnp.min(jnp.where(l == m, lane, float(e)), axis=-1, keepdims=True)
        hot = lane == ix
        vals.append(m)
        idxs.append(ix)
        hots.append(hot)
        l = jnp.where(hot, -jnp.inf, l)
    exps = [jnp.exp(v - vals[0]) for v in vals]
    denom = functools.reduce(lambda p, q: p + q, exps)
    sel = functools.reduce(lambda p, q: p + q, [h.astype(_F32) for h in hots])
    ti = lax.broadcasted_iota(jnp.int32, (tm, tm), 0)
    si = lax.broadcasted_iota(jnp.int32, (tm, tm), 1)
    before = _dot((si < ti).astype(_BF16), sel.astype(_BF16)) + carry_scr[...]
    slot = lax.broadcasted_iota(jnp.int32, (tm, TOP_K), 1)
    idx_out = jnp.zeros((tm, TOP_K), _F32)
    gate_out = jnp.zeros((tm, TOP_K), _F32)
    rank_out = jnp.zeros((tm, TOP_K), _F32)
    for k in range(TOP_K):
        rank_k = jnp.sum(jnp.where(hots[k], before, 0.0), axis=-1, keepdims=True)
        idx_out = jnp.where(slot == k, idxs[k], idx_out)
        gate_out = jnp.where(slot == k, exps[k] / denom, gate_out)
        rank_out = jnp.where(slot == k, rank_k, rank_out)
    idx_ref[...] = idx_out.astype(jnp.int32)
    gate_ref[...] = gate_out
    rank_ref[...] = rank_out.astype(jnp.int32)
    carry_scr[...] += jnp.sum(sel, axis=0, keepdims=True)
    cnt_ref[...] = carry_scr[...].astype(jnp.int32)


def _route(logits):
    n, e = logits.shape
    tm = _tile(n, 512, 8)
    slot = pl.BlockSpec((tm, TOP_K), lambda i: (i, 0))
    return pl.pallas_call(
        _route_body,
        grid=(n // tm,),
        in_specs=[pl.BlockSpec((tm, e), lambda i: (i, 0))],
        out_specs=[slot, slot, slot, pl.BlockSpec((1, e), lambda i: (0, 0))],
        out_shape=[jax.ShapeDtypeStruct((n, TOP_K), jnp.int32), jax.ShapeDtypeStruct((n, TOP_K), _F32),
                   jax.ShapeDtypeStruct((n, TOP_K), jnp.int32), jax.ShapeDtypeStruct((1, e), jnp.int32)],
        scratch_shapes=[pltpu.VMEM((1, e), _F32)],
        compiler_params=_params("arbitrary"),
        name="route",
    )(logits)


SUBLANES = 8


def _pad_chunks(bm):
    chunks, s = [], bm // 2
    while s >= SUBLANES:
        chunks.append((s, None))
        s //= 2
    return chunks + [(1, r) for r in range(SUBLANES - 1)]


def _dispatch_body(dest_ref, padlo_ref, padn_ref, nv_ref, h2_ref, xs_hbm, zero_scr, sem, zsem, *,
                   bm, n_experts):
    tm = h2_ref.shape[0]

    def issue(t, carry):
        for k in range(TOP_K):
            pltpu.make_async_copy(h2_ref.at[pl.ds(t, 1), :],
                                  xs_hbm.at[pl.ds(dest_ref[t * TOP_K + k], 1), :], sem).start()
        return carry

    lax.fori_loop(0, tm, issue, 0)

    def pad_copy(e, size, single):
        lo, n_pad = padlo_ref[e], padn_ref[e]
        if single is not None:
            on = single < (n_pad & (SUBLANES - 1))
            start = lo + single
        else:
            on = (n_pad & size) != 0
            above = (n_pad // (2 * size)) * (2 * size)
            start = pl.multiple_of(lo + n_pad - above - size, SUBLANES)
        return on, pltpu.make_async_copy(zero_scr.at[pl.ds(0, size), :],
                                         xs_hbm.at[pl.ds(start, size), :], zsem)

    @pl.when(pl.program_id(0) == 0)
    def _():
        zero_scr[...] = jnp.zeros_like(zero_scr)

        def start_pads(e, carry):
            for size, single in _pad_chunks(bm):
                on, cp = pad_copy(e, size, single)
                pl.when(on)(cp.start)
            return carry

        def wait_pads(e, carry):
            for size, single in _pad_chunks(bm):
                on, cp = pad_copy(e, size, single)
                pl.when(on)(cp.wait)
            return carry

        def tail_copy(h):
            return pltpu.make_async_copy(
                zero_scr, xs_hbm.at[pl.ds(pl.multiple_of(h * (bm // 2), SUBLANES), bm // 2), :], zsem)

        n_halves = xs_hbm.shape[0] // (bm // 2)
        lax.fori_loop(0, n_experts, start_pads, 0)
        lax.fori_loop(2 * nv_ref[0], n_halves, lambda h, c: (tail_copy(h).start(), c)[1], 0)
        lax.fori_loop(0, n_experts, wait_pads, 0)
        lax.fori_loop(2 * nv_ref[0], n_halves, lambda h, c: (tail_copy(h).wait(), c)[1], 0)

    for k in range(TOP_K):
        pltpu.make_async_copy(h2_ref, xs_hbm.at[pl.ds(0, tm), :], sem).wait()


def _dispatch(h2p, dest_flat, pad_lo, pad_n, n_valid, n_rows, bm):
    n, c = h2p.shape
    tm = _tile(n, 256, 8)
    smem = lambda shape, imap: pl.BlockSpec(shape, imap, memory_space=pltpu.SMEM)
    n_experts = pad_lo.shape[0]
    return pl.pallas_call(
        functools.partial(_dispatch_body, bm=bm, n_experts=n_experts),
        grid=(n // tm,),
        in_specs=[smem((tm * TOP_K,), lambda i: (i,)),
                  smem((n_experts,), lambda i: (0,)),
                  smem((n_experts,), lambda i: (0,)),
                  smem((1,), lambda i: (0,)),
                  pl.BlockSpec((tm, c), lambda i: (i, 0))],
        out_specs=pl.BlockSpec(memory_space=pl.ANY),
        out_shape=jax.ShapeDtypeStruct((n_rows, c), h2p.dtype),
        scratch_shapes=[pltpu.VMEM((bm // 2, c), h2p.dtype),
                        pltpu.SemaphoreType.DMA(()), pltpu.SemaphoreType.DMA(())],
        compiler_params=_params("arbitrary", row_dma_loop=True),
        name="dispatch",
    )(dest_flat, pad_lo, pad_n, n_valid, h2p)


def _first_of_expert(be_ref, nv_ref, i):
    changed = jnp.logical_or(i == 0, be_ref[i] != be_ref[jnp.maximum(i - 1, 0)])
    return jnp.logical_and(i < nv_ref[0], changed)


def _moe_up_body(be_ref, nv_ref, xs_ref, w1_ref, b1_ref, o_ref, w1b_scr, xb_scr):
    i = pl.program_id(0)
    j = pl.program_id(1)
    bm, tn = o_ref.shape
    valid = i < nv_ref[0]

    @pl.when(_first_of_expert(be_ref, nv_ref, i))
    def _():
        w1b_scr[j] = w1_ref[0].astype(_BF16)

    @pl.when(jnp.logical_and(valid, j == 0))
    def _():
        xb_scr[...] = xs_ref[...].astype(_BF16)

    @pl.when(valid)
    def _():
        h = _dot(xb_scr[...], w1b_scr[j]) + b1_ref[0]
        even = (lax.broadcasted_iota(jnp.int32, (bm, LANES), 1) & 1) == 0
        for g in range(tn // LANES):
            pa = h[:, (2 * g) * LANES:(2 * g + 1) * LANES]
            pb = h[:, (2 * g + 1) * LANES:(2 * g + 2) * LANES]
            glu = jnp.where(even, pa, pltpu.roll(pb, 1, 1))
            lin = jnp.where(even, pltpu.roll(pa, LANES - 1, 1), pb)
            glu = jnp.minimum(glu, SWIGLU_LIMIT)
            lin = jnp.clip(lin, -SWIGLU_LIMIT, SWIGLU_LIMIT)
            o_ref[:, g * LANES:(g + 1) * LANES] = (
                glu * _sigmoid(SWIGLU_ALPHA * glu) * (lin + 1.0)).astype(o_ref.dtype)

    @pl.when(jnp.logical_not(valid))
    def _():
        o_ref[...] = jnp.zeros_like(o_ref)


def _moe_down_body(be_ref, nv_ref, act_ref, w2_ref, b2_ref, y_ref, w2b_scr, perm_scr):
    i = pl.program_id(0)
    j = pl.program_id(1)
    _, f, tn = w2b_scr.shape
    valid = i < nv_ref[0]
    half = LANES // 2

    @pl.when(_first_of_expert(be_ref, nv_ref, i))
    def _():
        for cb in range(tn // LANES):
            cols = slice(cb * LANES, (cb + 1) * LANES)
            for g in range(f // LANES):
                base = g * LANES
                perm_scr[pl.ds(base, half, stride=2), :] = w2_ref[0, pl.ds(base, half), cols]
                perm_scr[pl.ds(base + 1, half, stride=2), :] = w2_ref[0, pl.ds(base + half, half), cols]
            w2b_scr[j, :, cols] = perm_scr[...].astype(_BF16)

    @pl.when(valid)
    def _():
        y_ref[...] = _dot(act_ref[...], w2b_scr[j]) + b2_ref[0]

    @pl.when(jnp.logical_not(valid))
    def _():
        y_ref[...] = jnp.zeros_like(y_ref)


def _moe_block_rows(assignments, n_experts):
    bm = 128
    while bm < 512 and bm * 2 * 4 <= assignments // n_experts:
        bm *= 2
    return bm


def _moe(xs, block_e, n_valid, w1, b1, w2, b2, bm):
    p, d = xs.shape
    e, _, f2 = w1.shape
    f = f2 // 2
    n_blocks = p // bm
    tn_up = _tile(f, 512, LANES)
    tn_dn = _tile(d, 1024, LANES)
    j_up, j_dn = f // tn_up, d // tn_dn

    def blk(i, nv):
        return jnp.minimum(i, nv[0] - 1)

    def tile_idx(n_j):
        return lambda i, j, be, nv: (be[blk(i, nv)], 0, jnp.where(_first_of_expert(be, nv, i), j, n_j - 1))

    act = pl.pallas_call(
        _moe_up_body,
        grid_spec=pltpu.PrefetchScalarGridSpec(
            num_scalar_prefetch=2,
            grid=(n_blocks, j_up),
            in_specs=[
                pl.BlockSpec((bm, d), lambda i, j, be, nv: (blk(i, nv), 0)),
                pl.BlockSpec((1, d, 2 * tn_up), tile_idx(j_up)),
                pl.BlockSpec((1, 1, 2 * tn_up), lambda i, j, be, nv: (be[blk(i, nv)], 0, j)),
            ],
            out_specs=pl.BlockSpec((bm, tn_up), lambda i, j, be, nv: (i, j)),
            scratch_shapes=[pltpu.VMEM((j_up, d, 2 * tn_up), _BF16), pltpu.VMEM((bm, d), _BF16)],
        ),
        out_shape=jax.ShapeDtypeStruct((p, f), _BF16),
        compiler_params=_params("arbitrary", "arbitrary"),
        name="moe_up",
    )(block_e, n_valid, xs, w1, b1.reshape(e, 1, f2))

    return pl.pallas_call(
        _moe_down_body,
        grid_spec=pltpu.PrefetchScalarGridSpec(
            num_scalar_prefetch=2,
            grid=(n_blocks, j_dn),
            in_specs=[
                pl.BlockSpec((bm, f), lambda i, j, be, nv: (blk(i, nv), 0)),
                pl.BlockSpec((1, f, tn_dn), tile_idx(j_dn)),
                pl.BlockSpec((1, 1, tn_dn), lambda i, j, be, nv: (be[blk(i, nv)], 0, j)),
            ],
            out_specs=pl.BlockSpec((bm, tn_dn), lambda i, j, be, nv: (i, j)),
            scratch_shapes=[pltpu.VMEM((j_dn, f, tn_dn), _BF16), pltpu.VMEM((f, LANES), _F32)],
        ),
        out_shape=jax.ShapeDtypeStruct((p, d), _F32),
        compiler_params=_params("arbitrary", "arbitrary"),
        name="moe_down",
    )(block_e, n_valid, act, w2, b2.reshape(e, 1, d))


def _combine_body(dest_ref, gate_ref, x1_ref, fw_ref, y_hbm, o_ref, y_scr, sem):
    tm, d = x1_ref.shape

    def issue(t, carry):
        for k in range(TOP_K):
            pltpu.make_async_copy(y_hbm.at[pl.ds(dest_ref[t * TOP_K + k], 1), :],
                                  y_scr.at[pl.ds(k * tm + t, 1), :], sem).start()
        return carry

    lax.fori_loop(0, tm, issue, 0)
    pltpu.make_async_copy(y_hbm.at[pl.ds(0, TOP_K * tm), :], y_scr, sem).wait()
    acc = x1_ref[...]
    gates = gate_ref[...]
    for k in range(TOP_K):
        acc = acc + y_scr[pl.ds(k * tm, tm), :] * gates[:, k:k + 1]
    o_ref[...] = _rms_scale(acc) * fw_ref[...]


def _combine(dest_flat, gates, x1, final_gain, y):
    n, d = x1.shape
    tm = _tile(n, 256, 8)
    return pl.pallas_call(
        _combine_body,
        grid=(n // tm,),
        in_specs=[pl.BlockSpec((tm * TOP_K,), lambda i: (i,), memory_space=pltpu.SMEM),
                  pl.BlockSpec((tm, TOP_K), lambda i: (i, 0)),
                  pl.BlockSpec((tm, d), lambda i: (i, 0)),
                  pl.BlockSpec((1, d), lambda i: (0, 0)),
                  pl.BlockSpec(memory_space=pl.ANY)],
        out_specs=pl.BlockSpec((tm, d), lambda i: (i, 0)),
        out_shape=jax.ShapeDtypeStruct((n, d), _F32),
        scratch_shapes=[pltpu.VMEM((TOP_K * tm, d), _F32), pltpu.SemaphoreType.DMA(())],
        compiler_params=_params("arbitrary", row_dma_loop=True),
        name="combine",
    )(dest_flat, gates, x1, final_gain, y)


def kernel(x, mix_norm, w_in, ln_v_w, ln_v_b, w_spatial, b_spatial, gmlp_out_norm, hgrn_lb_fwd,
           hgrn_lb_bwd, hgrn_out_norm, w_out, moe_norm, w_router, b_router, w1, b1, w2, b2, final_norm):
    batch, seq, d = x.shape
    depth = mix_norm.shape[0]
    assert depth == 1, "single trunk layer"
    layer = 0
    d_a = ln_v_w.shape[1]
    d_b = w_out.shape[1] - d_a
    f_b = hgrn_lb_fwd.shape[1]
    n_experts = w_router.shape[2]
    n = batch * seq
    assert seq % GMLP_CHUNK == 0 and seq % SCAN_CHUNK == 0 and d_a % LANES == 0 and d_a == d_b == f_b
    assert hgrn_out_norm.shape[1] == LANES and w_spatial.shape[2] == GMLP_CHUNK

    x2 = x.reshape(n, d)
    z = _in_proj(x2, mix_norm[layer][None], w_in[layer].astype(_BF16))
    a = _gmlp(z, ln_v_w[layer][None], ln_v_b[layer][None], w_spatial[layer].astype(_BF16),
              b_spatial[layer][:, :, None], gmlp_out_norm[layer][None], d_a)
    bo = _hgrn(z, hgrn_lb_fwd, hgrn_lb_bwd, hgrn_out_norm[layer][None], batch, seq, 2 * d_a, f_b, d_b, layer)
    w_out_bf = w_out[layer].astype(_BF16)
    x1, h2p, logits = _out_proj(a, bo, w_out_bf[:d_a], w_out_bf[d_a:], x2, moe_norm[layer][None],
                                w_router[layer].astype(_BF16), b_router[layer][None])

    idx, gates, rank, counts = _route(logits)

    assignments = n * TOP_K
    bm = _moe_block_rows(assignments, n_experts)
    n_blocks = -(-assignments // bm) + n_experts
    counts = counts[0]
    padded = (counts + bm - 1) // bm * bm
    pad_end = jnp.cumsum(padded)
    start_pad = pad_end - padded
    dest = (start_pad[idx] + rank).astype(jnp.int32).reshape(-1)
    block_start = jnp.arange(n_blocks, dtype=jnp.int32) * bm
    block_e = jnp.minimum(jnp.sum(block_start[:, None] >= pad_end[None, :], axis=1),
                          n_experts - 1).astype(jnp.int32)
    n_valid = (pad_end[-1:] // bm).astype(jnp.int32)

    xs = _dispatch(h2p, dest, (start_pad + counts).astype(jnp.int32), (padded - counts).astype(jnp.int32),
                   n_valid, n_blocks * bm, bm)
    y = _moe(xs, block_e, n_valid, w1[layer], b1[layer], w2[layer], b2[layer], bm)
    out = _combine(dest, gates, x1, final_norm[None], y)
    return out.reshape(batch, seq, d)
```

```python
import functools

import jax
import jax.numpy as jnp
from jax import lax
from jax.experimental import pallas as pl
from jax.experimental.pallas import tpu as pltpu

EPS = 1e-5
LANES = 128
GMLP_CHUNK = 128
SCAN_CHUNK = 64
TOP_K = 4
SWIGLU_LIMIT = 7.0
SWIGLU_ALPHA = 1.702
VMEM_LIMIT = 56 * 1024 * 1024

_F32 = jnp.float32
_BF16 = jnp.bfloat16


def _tile(n, target, align):
    if n <= target:
        return n
    t = target - target % align
    while t > align and n % t:
        t -= align
    assert n % t == 0, (n, target, align)
    return t


def _params(*semantics, row_dma_loop=False):
    return pltpu.CompilerParams(dimension_semantics=semantics, vmem_limit_bytes=VMEM_LIMIT,
                                disable_bounds_checks=row_dma_loop)


def _dot(a, b):
    return jnp.dot(a, b, preferred_element_type=_F32)


def _dot_nt(a, b):
    return lax.dot_general(a, b, (((1,), (1,)), ((), ())), preferred_element_type=_F32)


def _dot_tn(a, b):
    return lax.dot_general(a, b, (((0,), (0,)), ((), ())), preferred_element_type=_F32)


def _rms_scale(x):
    return x * lax.rsqrt(jnp.mean(x * x, axis=-1, keepdims=True) + EPS)


def _sigmoid(x):
    return jax.nn.sigmoid(x)


def _inproj_body(x_ref, g_ref, w_ref, o_ref, h_scr):
    @pl.when(pl.program_id(1) == 0)
    def _():
        h_scr[...] = (_rms_scale(x_ref[...]) * g_ref[...]).astype(h_scr.dtype)

    o_ref[...] = _dot(h_scr[...], w_ref[...]).astype(o_ref.dtype)


def _in_proj(x2, gain, w_bf):
    n, d = x2.shape
    c = w_bf.shape[1]
    tm = _tile(n, 1024, 8)
    tn = _tile(c, 1024, LANES)
    return pl.pallas_call(
        _inproj_body,
        grid=(n // tm, c // tn),
        in_specs=[pl.BlockSpec((tm, d), lambda i, j: (i, 0)),
                  pl.BlockSpec((1, d), lambda i, j: (0, 0)),
                  pl.BlockSpec((d, tn), lambda i, j: (0, j))],
        out_specs=pl.BlockSpec((tm, tn), lambda i, j: (i, j)),
        out_shape=jax.ShapeDtypeStruct((n, c), _BF16),
        scratch_shapes=[pltpu.VMEM((tm, d), _BF16)],
        compiler_params=_params("parallel", "arbitrary"),
        name="in_proj",
    )(x2, gain, w_bf)


def _gelu(x):
    return 0.5 * x * (1.0 + lax.erf(x * (2.0 ** -0.5)))


def _gmlp_body(u_ref, v_ref, lnw_ref, lnb_ref, ws_ref, bs_ref, gw_ref, o_ref, a_scr, *, n_heads):
    c = GMLP_CHUNK
    d_a = n_heads * LANES

    def chunk(ci, carry):
        r0 = pl.multiple_of(ci * c, c)
        u = _gelu(u_ref[pl.ds(r0, c), :].astype(_F32))
        v = _gelu(v_ref[pl.ds(r0, c), :].astype(_F32))
        mu = jnp.mean(v, axis=-1, keepdims=True)
        vc = v - mu
        var = jnp.mean(vc * vc, axis=-1, keepdims=True)
        vn = ((vc * lax.rsqrt(var + EPS)) * lnw_ref[...] + lnb_ref[...]).astype(_BF16)
        ss = jnp.zeros((c, 1), _F32)
        for h in range(n_heads):
            cols = slice(h * LANES, (h + 1) * LANES)
            mixed = _dot(ws_ref[h], vn[:, cols]) + bs_ref[h]
            a = u[:, cols] * mixed
            ss = ss + jnp.sum(a * a, axis=-1, keepdims=True)
            a_scr[:, cols] = a
        scale = lax.rsqrt(ss * (1.0 / d_a) + EPS)
        o_ref[pl.ds(r0, c), :] = (a_scr[...] * scale * gw_ref[...]).astype(o_ref.dtype)
        return carry

    lax.fori_loop(0, u_ref.shape[0] // c, chunk, 0, unroll=2)


def _gmlp(z, ln_w, ln_b, ws_bf, bs_col, out_gain, d_a):
    n = z.shape[0]
    n_heads = d_a // LANES
    rows = _tile(n, 512, GMLP_CHUNK)
    full = lambda shape: pl.BlockSpec(shape, lambda i: (0,) * len(shape))
    return pl.pallas_call(
        functools.partial(_gmlp_body, n_heads=n_heads),
        grid=(n // rows,),
        in_specs=[pl.BlockSpec((rows, d_a), lambda i: (i, 0)),
                  pl.BlockSpec((rows, d_a), lambda i: (i, 1)),
                  full((1, d_a)), full((1, d_a)),
                  full((n_heads, GMLP_CHUNK, GMLP_CHUNK)),
                  full((n_heads, GMLP_CHUNK, 1)),
                  full((1, d_a))],
        out_specs=pl.BlockSpec((rows, d_a), lambda i: (i, 0)),
        out_shape=jax.ShapeDtypeStruct((n, d_a), _BF16),
        scratch_shapes=[pltpu.VMEM((GMLP_CHUNK, d_a), _F32)],
        compiler_params=_params("parallel"),
        name="gmlp",
    )(z, z, ln_w, ln_b, ws_bf, bs_col, out_gain)


def _hgrn_body(q_ref, ff_ref, fb_ref, i_ref, g_ref, lbf_ref, lbb_ref, nw_ref, o_ref,
               of_scr, ob_scr, sf_scr, sb_scr, *, layer):
    c = SCAN_CHUNK
    seq = q_ref.shape[0]
    n_chunks = seq // c

    def lower_bound(t_ref):
        t = t_ref[...].astype(_F32)
        e = jnp.exp(t - jnp.max(t, axis=0, keepdims=True))
        sm = e / jnp.sum(e, axis=0, keepdims=True)
        return jnp.sum(sm[:layer + 1], axis=0, keepdims=True)

    lb_f = lower_bound(lbf_ref)
    lb_b = lower_bound(lbb_ref)
    sf_scr[...] = jnp.zeros_like(sf_scr)
    sb_scr[...] = jnp.zeros_like(sb_scr)

    row = lax.broadcasted_iota(jnp.int32, (c, LANES), 0)
    ti = lax.broadcasted_iota(jnp.int32, (c, c), 0)
    si = lax.broadcasted_iota(jnp.int32, (c, c), 1)

    def scan(x, reverse):
        d = 1
        while d < c:
            if reverse:
                x = x + jnp.where(row < c - d, pltpu.roll(x, c - d, 0), 0.0)
            else:
                x = x + jnp.where(row >= d, pltpu.roll(x, d, 0), 0.0)
            d *= 2
        return x

    def one_dir(r0, f_ref, lb, s_scr, out_scr, reverse):
        q = q_ref[pl.ds(r0, c), :].astype(_F32)
        q = q * _sigmoid(q)
        f = lb + (1.0 - lb) * _sigmoid(f_ref[pl.ds(r0, c), :].astype(_F32))
        k = 1.0 - f
        cum = scan(jnp.log(f), reverse)
        tot = cum[0:1, :] if reverse else cum[c - 1:c, :]
        qd = (q * jnp.exp(cum)).astype(_BF16)
        kd = (k * jnp.exp(-cum)).astype(_BF16)
        kt = (k * jnp.exp(tot - cum)).astype(_BF16)
        v = i_ref[pl.ds(r0, c), :]
        mask = (si >= ti) if reverse else (si <= ti)
        sc = jnp.where(mask, _dot_nt(qd, kd), 0.0).astype(_BF16)
        s_prev = s_scr[...]
        out_scr[pl.ds(r0, c), :] = _dot(sc, v) + _dot_nt(qd, s_prev.astype(_BF16))
        s_scr[...] = s_prev * jnp.exp(tot) + _dot_tn(v, kt)

    def step(ci, carry):
        one_dir(pl.multiple_of(ci * c, c), ff_ref, lb_f, sf_scr, of_scr, False)
        one_dir(pl.multiple_of((n_chunks - 1 - ci) * c, c), fb_ref, lb_b, sb_scr, ob_scr, True)
        return carry

    lax.fori_loop(0, n_chunks, step, 0, unroll=8)

    rb = _tile(seq, 256, c)

    def finish(bi, carry):
        r0 = pl.multiple_of(bi * rb, rb)
        o = _rms_scale(of_scr[pl.ds(r0, rb), :] + ob_scr[pl.ds(r0, rb), :])
        g = g_ref[pl.ds(r0, rb), :].astype(_F32)
        o_ref[pl.ds(r0, rb), :] = (o * nw_ref[...] * (g * _sigmoid(g))).astype(o_ref.dtype)
        return carry

    lax.fori_loop(0, seq // rb, finish, 0)


def _hgrn(z, lb_f, lb_b, norm_w, batch, seq, col0, f_b, d_b, layer):
    n_heads = d_b // LANES
    h0 = col0 // LANES
    kh = f_b // LANES
    zspec = lambda off: pl.BlockSpec((seq, LANES), lambda b, h: (b, off + h))
    depth1 = lb_f.shape[0]
    return pl.pallas_call(
        functools.partial(_hgrn_body, layer=layer),
        grid=(batch, n_heads),
        in_specs=[zspec(h0), zspec(h0 + kh), zspec(h0 + 2 * kh), zspec(h0 + 3 * kh),
                  zspec(h0 + 3 * kh + n_heads),
                  pl.BlockSpec((depth1, LANES), lambda b, h: (0, h)),
                  pl.BlockSpec((depth1, LANES), lambda b, h: (0, h)),
                  pl.BlockSpec((1, LANES), lambda b, h: (0, 0))],
        out_specs=pl.BlockSpec((seq, LANES), lambda b, h: (b, h)),
        out_shape=jax.ShapeDtypeStruct((batch * seq, d_b), _BF16),
        scratch_shapes=[pltpu.VMEM((seq, LANES), _F32), pltpu.VMEM((seq, LANES), _F32),
                        pltpu.VMEM((LANES, LANES), _F32), pltpu.VMEM((LANES, LANES), _F32)],
        compiler_params=_params("parallel", "parallel"),
        name="hgrn2",
    )(z, z, z, z, z, lb_f, lb_b, norm_w)


def _outproj_body(a_ref, b_ref, wa_ref, wb_ref, x_ref, mg_ref, wr_ref, br_ref,
                  x1_ref, h2_ref, lg_ref):
    x1 = x_ref[...] + _dot(a_ref[...], wa_ref[...]) + _dot(b_ref[...], wb_ref[...])
    x1_ref[...] = x1
    h2 = _rms_scale(x1) * mg_ref[...]
    h2_ref[...] = h2
    lg_ref[...] = _dot(h2.astype(_BF16), wr_ref[...]) + br_ref[...]


def _out_proj(a, b, wa_bf, wb_bf, x2, moe_gain, wr_bf, b_router):
    n, d = x2.shape
    d_a, d_b, e = a.shape[1], b.shape[1], wr_bf.shape[1]
    tm = _tile(n, 512, 8)
    full = lambda shape: pl.BlockSpec(shape, lambda i: (0,) * len(shape))
    row = lambda w: pl.BlockSpec((tm, w), lambda i: (i, 0))
    return pl.pallas_call(
        _outproj_body,
        grid=(n // tm,),
        in_specs=[row(d_a), row(d_b), full((d_a, d)), full((d_b, d)), row(d),
                  full((1, d)), full((d, e)), full((1, e))],
        out_specs=[row(d), row(d), row(e)],
        out_shape=[jax.ShapeDtypeStruct((n, d), _F32), jax.ShapeDtypeStruct((n, d), _F32),
                   jax.ShapeDtypeStruct((n, e), _F32)],
        compiler_params=_params("parallel"),
        name="out_proj",
    )(a, b, wa_bf, wb_bf, x2, moe_gain, wr_bf, b_router)


def _route_body(lg_ref, idx_ref, gate_ref, rank_ref, cnt_ref, carry_scr):
    @pl.when(pl.program_id(0) == 0)
    def _():
        carry_scr[...] = jnp.zeros_like(carry_scr)

    l = lg_ref[...]
    tm, e = l.shape
    lane = lax.broadcasted_iota(jnp.int32, (tm, e), 1).astype(_F32)
    vals, idxs, hots = [], [], []
    for _ in range(TOP_K):
        m = jnp.max(l, axis=-1, keepdims=True)
        ix = jnp.min(jnp.where(l == m, lane, float(e)), axis=-1, keepdims=True)
        hot = lane == ix
        vals.append(m)
        idxs.append(ix)
        hots.append(hot)
        l = jnp.where(hot, -jnp.inf, l)
    exps = [jnp.exp(v - vals[0]) for v in vals]
    denom = functools.reduce(lambda p, q: p + q, exps)
    sel = functools.reduce(lambda p, q: p + q, [h.astype(_F32) for h in hots])
    ti = lax.broadcasted_iota(jnp.int32, (tm, tm), 0)
    si = lax.broadcasted_iota(jnp.int32, (tm, tm), 1)
    before = _dot((si < ti).astype(_BF16), sel.astype(_BF16)) + carry_scr[...]
    slot = lax.broadcasted_iota(jnp.int32, (tm, TOP_K), 1)
    idx_out = jnp.zeros((tm, TOP_K), _F32)
    gate_out = jnp.zeros((tm, TOP_K), _F32)
    rank_out = jnp.zeros((tm, TOP_K), _F32)
    for k in range(TOP_K):
        rank_k = jnp.sum(jnp.where(hots[k], before, 0.0), axis=-1, keepdims=True)
        idx_out = jnp.where(slot == k, idxs[k], idx_out)
        gate_out = jnp.where(slot == k, exps[k] / denom, gate_out)
        rank_out = jnp.where(slot == k, rank_k, rank_out)
    idx_ref[...] = idx_out.astype(jnp.int32)
    gate_ref[...] = gate_out
    rank_ref[...] = rank_out.astype(jnp.int32)
    carry_scr[...] += jnp.sum(sel, axis=0, keepdims=True)
    cnt_ref[...] = carry_scr[...].astype(jnp.int32)


def _route(logits):
    n, e = logits.shape
    tm = _tile(n, 512, 8)
    slot = pl.BlockSpec((tm, TOP_K), lambda i: (i, 0))
    return pl.pallas_call(
        _route_body,
        grid=(n // tm,),
        in_specs=[pl.BlockSpec((tm, e), lambda i: (i, 0))],
        out_specs=[slot, slot, slot, pl.BlockSpec((1, e), lambda i: (0, 0))],
        out_shape=[jax.ShapeDtypeStruct((n, TOP_K), jnp.int32), jax.ShapeDtypeStruct((n, TOP_K), _F32),
                   jax.ShapeDtypeStruct((n, TOP_K), jnp.int32), jax.ShapeDtypeStruct((1, e), jnp.int32)],
        scratch_shapes=[pltpu.VMEM((1, e), _F32)],
        compiler_params=_params("arbitrary"),
        name="route",
    )(logits)


SUBLANES = 8


def _pad_chunks(bm):
    chunks, s = [], bm // 2
    while s >= SUBLANES:
        chunks.append((s, None))
        s //= 2
    return chunks + [(1, r) for r in range(SUBLANES - 1)]


def _dispatch_body(dest_ref, padlo_ref, padn_ref, nv_ref, h2_ref, xs_hbm, zero_scr, sem, zsem, *,
                   bm, n_experts):
    tm = h2_ref.shape[0]

    def issue(t, carry):
        for k in range(TOP_K):
            pltpu.make_async_copy(h2_ref.at[pl.ds(t, 1), :],
                                  xs_hbm.at[pl.ds(dest_ref[t * TOP_K + k], 1), :], sem).start()
        return carry

    lax.fori_loop(0, tm, issue, 0)

    def pad_copy(e, size, single):
        lo, n_pad = padlo_ref[e], padn_ref[e]
        if single is not None:
            on = single < (n_pad & (SUBLANES - 1))
            start = lo + single
        else:
            on = (n_pad & size) != 0
            above = (n_pad // (2 * size)) * (2 * size)
            start = pl.multiple_of(lo + n_pad - above - size, SUBLANES)
        return on, pltpu.make_async_copy(zero_scr.at[pl.ds(0, size), :],
                                         xs_hbm.at[pl.ds(start, size), :], zsem)

    @pl.when(pl.program_id(0) == 0)
    def _():
        zero_scr[...] = jnp.zeros_like(zero_scr)

        def start_pads(e, carry):
            for size, single in _pad_chunks(bm):
                on, cp = pad_copy(e, size, single)
                pl.when(on)(cp.start)
            return carry

        def wait_pads(e, carry):
            for size, single in _pad_chunks(bm):
                on, cp = pad_copy(e, size, single)
                pl.when(on)(cp.wait)
            return carry

        def tail_copy(h):
            return pltpu.make_async_copy(
                zero_scr, xs_hbm.at[pl.ds(pl.multiple_of(h * (bm // 2), SUBLANES), bm // 2), :], zsem)

        n_halves = xs_hbm.shape[0] // (bm // 2)
        lax.fori_loop(0, n_experts, start_pads, 0)
        lax.fori_loop(2 * nv_ref[0], n_halves, lambda h, c: (tail_copy(h).start(), c)[1], 0)
        lax.fori_loop(0, n_experts, wait_pads, 0)
        lax.fori_loop(2 * nv_ref[0], n_halves, lambda h, c: (tail_copy(h).wait(), c)[1], 0)

    for k in range(TOP_K):
        pltpu.make_async_copy(h2_ref, xs_hbm.at[pl.ds(0, tm), :], sem).wait()


def _dispatch(h2p, dest_flat, pad_lo, pad_n, n_valid, n_rows, bm):
    n, c = h2p.shape
    tm = _tile(n, 256, 8)
    smem = lambda shape, imap: pl.BlockSpec(shape, imap, memory_space=pltpu.SMEM)
    n_experts = pad_lo.shape[0]
    return pl.pallas_call(
        functools.partial(_dispatch_body, bm=bm, n_experts=n_experts),
        grid=(n // tm,),
        in_specs=[smem((tm * TOP_K,), lambda i: (i,)),
                  smem((n_experts,), lambda i: (0,)),
                  smem((n_experts,), lambda i: (0,)),
                  smem((1,), lambda i: (0,)),
                  pl.BlockSpec((tm, c), lambda i: (i, 0))],
        out_specs=pl.BlockSpec(memory_space=pl.ANY),
        out_shape=jax.ShapeDtypeStruct((n_rows, c), h2p.dtype),
        scratch_shapes=[pltpu.VMEM((bm // 2, c), h2p.dtype),
                        pltpu.SemaphoreType.DMA(()), pltpu.SemaphoreType.DMA(())],
        compiler_params=_params("arbitrary", row_dma_loop=True),
        name="dispatch",
    )(dest_flat, pad_lo, pad_n, n_valid, h2p)


def _first_of_expert(be_ref, nv_ref, i):
    changed = jnp.logical_or(i == 0, be_ref[i] != be_ref[jnp.maximum(i - 1, 0)])
    return jnp.logical_and(i < nv_ref[0], changed)


def _moe_up_body(be_ref, nv_ref, xs_ref, w1_ref, b1_ref, o_ref, w1b_scr, xb_scr):
    i = pl.program_id(0)
    j = pl.program_id(1)
    bm, tn = o_ref.shape
    valid = i < nv_ref[0]

    @pl.when(_first_of_expert(be_ref, nv_ref, i))
    def _():
        w1b_scr[j] = w1_ref[0].astype(_BF16)

    @pl.when(jnp.logical_and(valid, j == 0))
    def _():
        xb_scr[...] = xs_ref[...].astype(_BF16)

    @pl.when(valid)
    def _():
        even = (lax.broadcasted_iota(jnp.int32, (bm, LANES), 1) & 1) == 0
        xb = xb_scr[...]
        for g in range(tn // LANES):
            if g % 2 == 0:
                cols = pl.ds(2 * g * LANES, 4 * LANES)
                h = _dot(xb, w1b_scr[j, :, cols]) + b1_ref[0, :, cols]
            pa = h[:, (2 * (g % 2)) * LANES:(2 * (g % 2) + 1) * LANES]
            pb = h[:, (2 * (g % 2) + 1) * LANES:(2 * (g % 2) + 2) * LANES]
            glu = jnp.where(even, pa, pltpu.roll(pb, 1, 1))
            lin = jnp.where(even, pltpu.roll(pa, LANES - 1, 1), pb)
            glu = jnp.minimum(glu, SWIGLU_LIMIT)
            lin = jnp.clip(lin, -SWIGLU_LIMIT, SWIGLU_LIMIT)
            o_ref[:, g * LANES:(g + 1) * LANES] = (
                glu * _sigmoid(SWIGLU_ALPHA * glu) * (lin + 1.0)).astype(o_ref.dtype)

    @pl.when(jnp.logical_not(valid))
    def _():
        o_ref[...] = jnp.zeros_like(o_ref)


def _moe_down_body(be_ref, nv_ref, act_ref, w2_ref, b2_ref, y_ref, w2b_scr, perm_scr):
    i = pl.program_id(0)
    j = pl.program_id(1)
    _, f, tn = w2b_scr.shape
    valid = i < nv_ref[0]
    half = LANES // 2

    @pl.when(_first_of_expert(be_ref, nv_ref, i))
    def _():
        for cb in range(tn // LANES):
            cols = slice(cb * LANES, (cb + 1) * LANES)
            for g in range(f // LANES):
                base = g * LANES
                perm_scr[pl.ds(base, half, stride=2), :] = w2_ref[0, pl.ds(base, half), cols]
                perm_scr[pl.ds(base + 1, half, stride=2), :] = w2_ref[0, pl.ds(base + half, half), cols]
            w2b_scr[j, :, cols] = perm_scr[...].astype(_BF16)

    @pl.when(valid)
    def _():
        y_ref[...] = _dot(act_ref[...], w2b_scr[j]) + b2_ref[0]

    @pl.when(jnp.logical_not(valid))
    def _():
        y_ref[...] = jnp.zeros_like(y_ref)


def _moe_block_rows(assignments, n_experts):
    bm = 128
    while bm < 512 and bm * 2 * 4 <= assignments // n_experts:
        bm *= 2
    return bm


def _moe(xs, block_e, n_valid, w1, b1, w2, b2, bm):
    p, d = xs.shape
    e, _, f2 = w1.shape
    f = f2 // 2
    n_blocks = p // bm
    tn_up = _tile(f, 512, LANES)
    tn_dn = _tile(d, 1024, LANES)
    j_up, j_dn = f // tn_up, d // tn_dn
    assert tn_up % (2 * LANES) == 0, "moe-up computes 256 hidden units per dot"

    def blk(i, nv):
        return jnp.minimum(i, nv[0] - 1)

    def tile_idx(n_j):
        return lambda i, j, be, nv: (be[blk(i, nv)], 0, jnp.where(_first_of_expert(be, nv, i), j, n_j - 1))

    act = pl.pallas_call(
        _moe_up_body,
        grid_spec=pltpu.PrefetchScalarGridSpec(
            num_scalar_prefetch=2,
            grid=(n_blocks, j_up),
            in_specs=[
                pl.BlockSpec((bm, d), lambda i, j, be, nv: (blk(i, nv), 0)),
                pl.BlockSpec((1, d, 2 * tn_up), tile_idx(j_up)),
                pl.BlockSpec((1, 1, 2 * tn_up), lambda i, j, be, nv: (be[blk(i, nv)], 0, j)),
            ],
            out_specs=pl.BlockSpec((bm, tn_up), lambda i, j, be, nv: (i, j)),
            scratch_shapes=[pltpu.VMEM((j_up, d, 2 * tn_up), _BF16), pltpu.VMEM((bm, d), _BF16)],
        ),
        out_shape=jax.ShapeDtypeStruct((p, f), _BF16),
        compiler_params=_params("arbitrary", "arbitrary"),
        name="moe_up",
    )(block_e, n_valid, xs, w1, b1.reshape(e, 1, f2))

    return pl.pallas_call(
        _moe_down_body,
        grid_spec=pltpu.PrefetchScalarGridSpec(
            num_scalar_prefetch=2,
            grid=(n_blocks, j_dn),
            in_specs=[
                pl.BlockSpec((bm, f), lambda i, j, be, nv: (blk(i, nv), 0)),
                pl.BlockSpec((1, f, tn_dn), tile_idx(j_dn)),
                pl.BlockSpec((1, 1, tn_dn), lambda i, j, be, nv: (be[blk(i, nv)], 0, j)),
            ],
            out_specs=pl.BlockSpec((bm, tn_dn), lambda i, j, be, nv: (i, j)),
            scratch_shapes=[pltpu.VMEM((j_dn, f, tn_dn), _BF16), pltpu.VMEM((f, LANES), _F32)],
        ),
        out_shape=jax.ShapeDtypeStruct((p, d), _F32),
        compiler_params=_params("arbitrary", "arbitrary"),
        name="moe_down",
    )(block_e, n_valid, act, w2, b2.reshape(e, 1, d))


def _combine_body(dest_ref, gate_ref, x1_ref, fw_ref, y_hbm, o_ref, y_scr, sem):
    tm, d = x1_ref.shape

    def issue(t, carry):
        for k in range(TOP_K):
            pltpu.make_async_copy(y_hbm.at[pl.ds(dest_ref[t * TOP_K + k], 1), :],
                                  y_scr.at[pl.ds(k * tm + t, 1), :], sem).start()
        return carry

    lax.fori_loop(0, tm, issue, 0)
    pltpu.make_async_copy(y_hbm.at[pl.ds(0, TOP_K * tm), :], y_scr, sem).wait()
    acc = x1_ref[...]
    gates = gate_ref[...]
    for k in range(TOP_K):
        acc = acc + y_scr[pl.ds(k * tm, tm), :] * gates[:, k:k + 1]
    o_ref[...] = _rms_scale(acc) * fw_ref[...]


def _combine(dest_flat, gates, x1, final_gain, y):
    n, d = x1.shape
    tm = _tile(n, 256, 8)
    return pl.pallas_call(
        _combine_body,
        grid=(n // tm,),
        in_specs=[pl.BlockSpec((tm * TOP_K,), lambda i: (i,), memory_space=pltpu.SMEM),
                  pl.BlockSpec((tm, TOP_K), lambda i: (i, 0)),
                  pl.BlockSpec((tm, d), lambda i: (i, 0)),
                  pl.BlockSpec((1, d), lambda i: (0, 0)),
                  pl.BlockSpec(memory_space=pl.ANY)],
        out_specs=pl.BlockSpec((tm, d), lambda i: (i, 0)),
        out_shape=jax.ShapeDtypeStruct((n, d), _F32),
        scratch_shapes=[pltpu.VMEM((TOP_K * tm, d), _F32), pltpu.SemaphoreType.DMA(())],
        compiler_params=_params("arbitrary", row_dma_loop=True),
        name="combine",
    )(dest_flat, gates, x1, final_gain, y)


def kernel(x, mix_norm, w_in, ln_v_w, ln_v_b, w_spatial, b_spatial, gmlp_out_norm, hgrn_lb_fwd,
           hgrn_lb_bwd, hgrn_out_norm, w_out, moe_norm, w_router, b_router, w1, b1, w2, b2, final_norm):
    batch, seq, d = x.shape
    depth = mix_norm.shape[0]
    assert depth == 1, "single trunk layer"
    layer = 0
    d_a = ln_v_w.shape[1]
    d_b = w_out.shape[1] - d_a
    f_b = hgrn_lb_fwd.shape[1]
    n_experts = w_router.shape[2]
    n = batch * seq
    assert seq % GMLP_CHUNK == 0 and seq % SCAN_CHUNK == 0 and d_a % LANES == 0 and d_a == d_b == f_b
    assert hgrn_out_norm.shape[1] == LANES and w_spatial.shape[2] == GMLP_CHUNK

    x2 = x.reshape(n, d)
    z = _in_proj(x2, mix_norm[layer][None], w_in[layer].astype(_BF16))
    a = _gmlp(z, ln_v_w[layer][None], ln_v_b[layer][None], w_spatial[layer].astype(_BF16),
              b_spatial[layer][:, :, None], gmlp_out_norm[layer][None], d_a)
    bo = _hgrn(z, hgrn_lb_fwd, hgrn_lb_bwd, hgrn_out_norm[layer][None], batch, seq, 2 * d_a, f_b, d_b, layer)
    w_out_bf = w_out[layer].astype(_BF16)
    x1, h2p, logits = _out_proj(a, bo, w_out_bf[:d_a], w_out_bf[d_a:], x2, moe_norm[layer][None],
                                w_router[layer].astype(_BF16), b_router[layer][None])

    idx, gates, rank, counts = _route(logits)

    assignments = n * TOP_K
    bm = _moe_block_rows(assignments, n_experts)
    n_blocks = -(-assignments // bm) + n_experts
    counts = counts[0]
    padded = (counts + bm - 1) // bm * bm
    pad_end = jnp.cumsum(padded)
    start_pad = pad_end - padded
    dest = (start_pad[idx] + rank).astype(jnp.int32).reshape(-1)
    block_start = jnp.arange(n_blocks, dtype=jnp.int32) * bm
    block_e = jnp.minimum(jnp.sum(block_start[:, None] >= pad_end[None, :], axis=1),
                          n_experts - 1).astype(jnp.int32)
    n_valid = (pad_end[-1:] // bm).astype(jnp.int32)

    xs = _dispatch(h2p, dest, (start_pad + counts).astype(jnp.int32), (padded - counts).astype(jnp.int32),
                   n_valid, n_blocks * bm, bm)
    y = _moe(xs, block_e, n_valid, w1[layer], b1[layer], w2[layer], b2[layer], bm)
    out = _combine(dest, gates, x1, final_norm[None], y)
    return out.reshape(batch, seq, d)
```

```python
import functools

import jax
import jax.numpy as jnp
from jax import lax
from jax.experimental import pallas as pl
from jax.experimental.pallas import tpu as pltpu

EPS = 1e-5
LANES = 128
GMLP_CHUNK = 128
SCAN_CHUNK = 64
TOP_K = 4
SWIGLU_LIMIT = 7.0
SWIGLU_ALPHA = 1.702
VMEM_LIMIT = 56 * 1024 * 1024

_F32 = jnp.float32
_BF16 = jnp.bfloat16


def _tile(n, target, align):
    if n <= target:
        return n
    t = target - target % align
    while t > align and n % t:
        t -= align
    assert n % t == 0, (n, target, align)
    return t


def _params(*semantics, row_dma_loop=False):
    return pltpu.CompilerParams(dimension_semantics=semantics, vmem_limit_bytes=VMEM_LIMIT,
                                disable_bounds_checks=row_dma_loop)


def _dot(a, b):
    return jnp.dot(a, b, preferred_element_type=_F32)


def _dot_nt(a, b):
    return lax.dot_general(a, b, (((1,), (1,)), ((), ())), preferred_element_type=_F32)


def _dot_tn(a, b):
    return lax.dot_general(a, b, (((0,), (0,)), ((), ())), preferred_element_type=_F32)


def _rms_scale(x):
    return x * lax.rsqrt(jnp.mean(x * x, axis=-1, keepdims=True) + EPS)


def _sigmoid(x):
    return jax.nn.sigmoid(x)


def _inproj_body(x_ref, g_ref, w_ref, o_ref, h_scr):
    @pl.when(pl.program_id(1) == 0)
    def _():
        h_scr[...] = (_rms_scale(x_ref[...]) * g_ref[...]).astype(h_scr.dtype)

    o_ref[...] = _dot(h_scr[...], w_ref[...]).astype(o_ref.dtype)


def _in_proj(x2, gain, w_bf):
    n, d = x2.shape
    c = w_bf.shape[1]
    tm = _tile(n, 1024, 8)
    tn = _tile(c, 1024, LANES)
    return pl.pallas_call(
        _inproj_body,
        grid=(n // tm, c // tn),
        in_specs=[pl.BlockSpec((tm, d), lambda i, j: (i, 0)),
                  pl.BlockSpec((1, d), lambda i, j: (0, 0)),
                  pl.BlockSpec((d, tn), lambda i, j: (0, j))],
        out_specs=pl.BlockSpec((tm, tn), lambda i, j: (i, j)),
        out_shape=jax.ShapeDtypeStruct((n, c), _BF16),
        scratch_shapes=[pltpu.VMEM((tm, d), _BF16)],
        compiler_params=_params("parallel", "arbitrary"),
        name="in_proj",
    )(x2, gain, w_bf)


def _gelu(x):
    return 0.5 * x * (1.0 + lax.erf(x * (2.0 ** -0.5)))


def _gmlp_body(u_ref, v_ref, lnw_ref, lnb_ref, ws_ref, bs_ref, gw_ref, o_ref, a_scr, *, n_heads):
    c = GMLP_CHUNK
    d_a = n_heads * LANES

    def chunk(ci, carry):
        r0 = pl.multiple_of(ci * c, c)
        u = _gelu(u_ref[pl.ds(r0, c), :].astype(_F32))
        v = _gelu(v_ref[pl.ds(r0, c), :].astype(_F32))
        mu = jnp.mean(v, axis=-1, keepdims=True)
        vc = v - mu
        var = jnp.mean(vc * vc, axis=-1, keepdims=True)
        vn = ((vc * lax.rsqrt(var + EPS)) * lnw_ref[...] + lnb_ref[...]).astype(_BF16)
        ss = jnp.zeros((c, 1), _F32)
        for h in range(n_heads):
            cols = slice(h * LANES, (h + 1) * LANES)
            mixed = _dot(ws_ref[h], vn[:, cols]) + bs_ref[h]
            a = u[:, cols] * mixed
            ss = ss + jnp.sum(a * a, axis=-1, keepdims=True)
            a_scr[:, cols] = a
        scale = lax.rsqrt(ss * (1.0 / d_a) + EPS)
        o_ref[pl.ds(r0, c), :] = (a_scr[...] * scale * gw_ref[...]).astype(o_ref.dtype)
        return carry

    lax.fori_loop(0, u_ref.shape[0] // c, chunk, 0)


def _gmlp(z, ln_w, ln_b, ws_bf, bs_col, out_gain, d_a):
    n = z.shape[0]
    n_heads = d_a // LANES
    rows = _tile(n, 512, GMLP_CHUNK)
    full = lambda shape: pl.BlockSpec(shape, lambda i: (0,) * len(shape))
    return pl.pallas_call(
        functools.partial(_gmlp_body, n_heads=n_heads),
        grid=(n // rows,),
        in_specs=[pl.BlockSpec((rows, d_a), lambda i: (i, 0)),
                  pl.BlockSpec((rows, d_a), lambda i: (i, 1)),
                  full((1, d_a)), full((1, d_a)),
                  full((n_heads, GMLP_CHUNK, GMLP_CHUNK)),
                  full((n_heads, GMLP_CHUNK, 1)),
                  full((1, d_a))],
        out_specs=pl.BlockSpec((rows, d_a), lambda i: (i, 0)),
        out_shape=jax.ShapeDtypeStruct((n, d_a), _BF16),
        scratch_shapes=[pltpu.VMEM((GMLP_CHUNK, d_a), _F32)],
        compiler_params=_params("parallel"),
        name="gmlp",
    )(z, z, ln_w, ln_b, ws_bf, bs_col, out_gain)


def _hgrn_body(q_ref, ff_ref, fb_ref, i_ref, g_ref, lbf_ref, lbb_ref, nw_ref, o_ref,
               of_scr, ob_scr, sf_scr, sb_scr, *, layer):
    c = SCAN_CHUNK
    seq = q_ref.shape[0]
    n_chunks = seq // c

    def lower_bound(t_ref):
        t = t_ref[...].astype(_F32)
        e = jnp.exp(t - jnp.max(t, axis=0, keepdims=True))
        sm = e / jnp.sum(e, axis=0, keepdims=True)
        return jnp.sum(sm[:layer + 1], axis=0, keepdims=True)

    lb_f = lower_bound(lbf_ref)
    lb_b = lower_bound(lbb_ref)
    sf_scr[...] = jnp.zeros_like(sf_scr)
    sb_scr[...] = jnp.zeros_like(sb_scr)

    row = lax.broadcasted_iota(jnp.int32, (c, LANES), 0)
    ti = lax.broadcasted_iota(jnp.int32, (c, c), 0)
    si = lax.broadcasted_iota(jnp.int32, (c, c), 1)

    def scan(x, reverse):
        d = 1
        while d < c:
            if reverse:
                x = x + jnp.where(row < c - d, pltpu.roll(x, c - d, 0), 0.0)
            else:
                x = x + jnp.where(row >= d, pltpu.roll(x, d, 0), 0.0)
            d *= 2
        return x

    def one_dir(r0, f_ref, lb, s_scr, out_scr, reverse):
        q = q_ref[pl.ds(r0, c), :].astype(_F32)
        q = q * _sigmoid(q)
        f = lb + (1.0 - lb) * _sigmoid(f_ref[pl.ds(r0, c), :].astype(_F32))
        k = 1.0 - f
        cum = scan(jnp.log(f), reverse)
        tot = cum[0:1, :] if reverse else cum[c - 1:c, :]
        qd = (q * jnp.exp(cum)).astype(_BF16)
        kd = (k * jnp.exp(-cum)).astype(_BF16)
        kt = (k * jnp.exp(tot - cum)).astype(_BF16)
        v = i_ref[pl.ds(r0, c), :]
        mask = (si >= ti) if reverse else (si <= ti)
        sc = jnp.where(mask, _dot_nt(qd, kd), 0.0).astype(_BF16)
        s_prev = s_scr[...]
        out_scr[pl.ds(r0, c), :] = _dot(sc, v) + _dot_nt(qd, s_prev.astype(_BF16))
        s_scr[...] = s_prev * jnp.exp(tot) + _dot_tn(v, kt)

    def step(ci, carry):
        one_dir(pl.multiple_of(ci * c, c), ff_ref, lb_f, sf_scr, of_scr, False)
        one_dir(pl.multiple_of((n_chunks - 1 - ci) * c, c), fb_ref, lb_b, sb_scr, ob_scr, True)
        return carry

    lax.fori_loop(0, n_chunks, step, 0, unroll=8)

    rb = _tile(seq, 256, c)

    def finish(bi, carry):
        r0 = pl.multiple_of(bi * rb, rb)
        o = _rms_scale(of_scr[pl.ds(r0, rb), :] + ob_scr[pl.ds(r0, rb), :])
        g = g_ref[pl.ds(r0, rb), :].astype(_F32)
        o_ref[pl.ds(r0, rb), :] = (o * nw_ref[...] * (g * _sigmoid(g))).astype(o_ref.dtype)
        return carry

    lax.fori_loop(0, seq // rb, finish, 0)


def _hgrn(z, lb_f, lb_b, norm_w, batch, seq, col0, f_b, d_b, layer):
    n_heads = d_b // LANES
    h0 = col0 // LANES
    kh = f_b // LANES
    zspec = lambda off: pl.BlockSpec((seq, LANES), lambda b, h: (b, off + h))
    depth1 = lb_f.shape[0]
    return pl.pallas_call(
        functools.partial(_hgrn_body, layer=layer),
        grid=(batch, n_heads),
        in_specs=[zspec(h0), zspec(h0 + kh), zspec(h0 + 2 * kh), zspec(h0 + 3 * kh),
                  zspec(h0 + 3 * kh + n_heads),
                  pl.BlockSpec((depth1, LANES), lambda b, h: (0, h)),
                  pl.BlockSpec((depth1, LANES), lambda b, h: (0, h)),
                  pl.BlockSpec((1, LANES), lambda b, h: (0, 0))],
        out_specs=pl.BlockSpec((seq, LANES), lambda b, h: (b, h)),
        out_shape=jax.ShapeDtypeStruct((batch * seq, d_b), _BF16),
        scratch_shapes=[pltpu.VMEM((seq, LANES), _F32), pltpu.VMEM((seq, LANES), _F32),
                        pltpu.VMEM((LANES, LANES), _F32), pltpu.VMEM((LANES, LANES), _F32)],
        compiler_params=_params("parallel", "parallel"),
        name="hgrn2",
    )(z, z, z, z, z, lb_f, lb_b, norm_w)


def _outproj_body(a_ref, b_ref, wa_ref, wb_ref, x_ref, mg_ref, wr_ref, br_ref,
                  x1_ref, h2_ref, lg_ref):
    x1 = x_ref[...] + _dot(a_ref[...], wa_ref[...]) + _dot(b_ref[...], wb_ref[...])
    x1_ref[...] = x1
    h2 = _rms_scale(x1) * mg_ref[...]
    h2_ref[...] = h2
    lg_ref[...] = _dot(h2.astype(_BF16), wr_ref[...]) + br_ref[...]


def _out_proj(a, b, wa_bf, wb_bf, x2, moe_gain, wr_bf, b_router):
    n, d = x2.shape
    d_a, d_b, e = a.shape[1], b.shape[1], wr_bf.shape[1]
    tm = _tile(n, 512, 8)
    full = lambda shape: pl.BlockSpec(shape, lambda i: (0,) * len(shape))
    row = lambda w: pl.BlockSpec((tm, w), lambda i: (i, 0))
    return pl.pallas_call(
        _outproj_body,
        grid=(n // tm,),
        in_specs=[row(d_a), row(d_b), full((d_a, d)), full((d_b, d)), row(d),
                  full((1, d)), full((d, e)), full((1, e))],
        out_specs=[row(d), row(d), row(e)],
        out_shape=[jax.ShapeDtypeStruct((n, d), _F32), jax.ShapeDtypeStruct((n, d), _F32),
                   jax.ShapeDtypeStruct((n, e), _F32)],
        compiler_params=_params("parallel"),
        name="out_proj",
    )(a, b, wa_bf, wb_bf, x2, moe_gain, wr_bf, b_router)


def _route_body(lg_ref, idx_ref, gate_ref, rank_ref, cnt_ref, carry_scr):
    @pl.when(pl.program_id(0) == 0)
    def _():
        carry_scr[...] = jnp.zeros_like(carry_scr)

    l = lg_ref[...]
    tm, e = l.shape
    lane = lax.broadcasted_iota(jnp.int32, (tm, e), 1).astype(_F32)
    vals, idxs, hots = [], [], []
    for _ in range(TOP_K):
        m = jnp.max(l, axis=-1, keepdims=True)
        ix = jnp.min(jnp.where(l == m, lane, float(e)), axis=-1, keepdims=True)
        hot = lane == ix
        vals.append(m)
        idxs.append(ix)
        hots.append(hot)
        l = jnp.where(hot, -jnp.inf, l)
    exps = [jnp.exp(v - vals[0]) for v in vals]
    denom = functools.reduce(lambda p, q: p + q, exps)
    sel = functools.reduce(lambda p, q: p + q, [h.astype(_F32) for h in hots])
    ti = lax.broadcasted_iota(jnp.int32, (tm, tm), 0)
    si = lax.broadcasted_iota(jnp.int32, (tm, tm), 1)
    before = _dot((si < ti).astype(_BF16), sel.astype(_BF16)) + carry_scr[...]
    slot = lax.broadcasted_iota(jnp.int32, (tm, TOP_K), 1)
    idx_out = jnp.zeros((tm, TOP_K), _F32)
    gate_out = jnp.zeros((tm, TOP_K), _F32)
    rank_out = jnp.zeros((tm, TOP_K), _F32)
    for k in range(TOP_K):
        rank_k = jnp.sum(jnp.where(hots[k], before, 0.0), axis=-1, keepdims=True)
        idx_out = jnp.where(slot == k, idxs[k], idx_out)
        gate_out = jnp.where(slot == k, exps[k] / denom, gate_out)
        rank_out = jnp.where(slot == k, rank_k, rank_out)
    idx_ref[...] = idx_out.astype(jnp.int32)
    gate_ref[...] = gate_out
    rank_ref[...] = rank_out.astype(jnp.int32)
    carry_scr[...] += jnp.sum(sel, axis=0, keepdims=True)
    cnt_ref[...] = carry_scr[...].astype(jnp.int32)


def _route(logits):
    n, e = logits.shape
    tm = _tile(n, 512, 8)
    slot = pl.BlockSpec((tm, TOP_K), lambda i: (i, 0))
    return pl.pallas_call(
        _route_body,
        grid=(n // tm,),
        in_specs=[pl.BlockSpec((tm, e), lambda i: (i, 0))],
        out_specs=[slot, slot, slot, pl.BlockSpec((1, e), lambda i: (0, 0))],
        out_shape=[jax.ShapeDtypeStruct((n, TOP_K), jnp.int32), jax.ShapeDtypeStruct((n, TOP_K), _F32),
                   jax.ShapeDtypeStruct((n, TOP_K), jnp.int32), jax.ShapeDtypeStruct((1, e), jnp.int32)],
        scratch_shapes=[pltpu.VMEM((1, e), _F32)],
        compiler_params=_params("arbitrary"),
        name="route",
    )(logits)


SUBLANES = 8


def _pad_chunks(bm):
    chunks, s = [], bm // 2
    while s >= SUBLANES:
        chunks.append((s, None))
        s //= 2
    return chunks + [(1, r) for r in range(SUBLANES - 1)]


def _dispatch_body(dest_ref, padlo_ref, padn_ref, nv_ref, h2_ref, xs_hbm, zero_scr, sem, zsem, *,
                   bm, n_experts):
    tm = h2_ref.shape[0]

    def issue(t, carry):
        for k in range(TOP_K):
            pltpu.make_async_copy(h2_ref.at[pl.ds(t, 1), :],
                                  xs_hbm.at[pl.ds(dest_ref[t * TOP_K + k], 1), :], sem).start()
        return carry

    lax.fori_loop(0, tm, issue, 0)

    def pad_copy(e, size, single):
        lo, n_pad = padlo_ref[e], padn_ref[e]
        if single is not None:
            on = single < (n_pad & (SUBLANES - 1))
            start = lo + single
        else:
            on = (n_pad & size) != 0
            above = (n_pad // (2 * size)) * (2 * size)
            start = pl.multiple_of(lo + n_pad - above - size, SUBLANES)
        return on, pltpu.make_async_copy(zero_scr.at[pl.ds(0, size), :],
                                         xs_hbm.at[pl.ds(start, size), :], zsem)

    @pl.when(pl.program_id(0) == 0)
    def _():
        zero_scr[...] = jnp.zeros_like(zero_scr)

        def start_pads(e, carry):
            for size, single in _pad_chunks(bm):
                on, cp = pad_copy(e, size, single)
                pl.when(on)(cp.start)
            return carry

        def wait_pads(e, carry):
            for size, single in _pad_chunks(bm):
                on, cp = pad_copy(e, size, single)
                pl.when(on)(cp.wait)
            return carry

        def tail_copy(h):
            return pltpu.make_async_copy(
                zero_scr, xs_hbm.at[pl.ds(pl.multiple_of(h * (bm // 2), SUBLANES), bm // 2), :], zsem)

        n_halves = xs_hbm.shape[0] // (bm // 2)
        lax.fori_loop(0, n_experts, start_pads, 0)
        lax.fori_loop(2 * nv_ref[0], n_halves, lambda h, c: (tail_copy(h).start(), c)[1], 0)
        lax.fori_loop(0, n_experts, wait_pads, 0)
        lax.fori_loop(2 * nv_ref[0], n_halves, lambda h, c: (tail_copy(h).wait(), c)[1], 0)

    for k in range(TOP_K):
        pltpu.make_async_copy(h2_ref, xs_hbm.at[pl.ds(0, tm), :], sem).wait()


def _dispatch(h2p, dest_flat, pad_lo, pad_n, n_valid, n_rows, bm):
    n, c = h2p.shape
    tm = _tile(n, 512, 8)
    smem = lambda shape, imap: pl.BlockSpec(shape, imap, memory_space=pltpu.SMEM)
    n_experts = pad_lo.shape[0]
    return pl.pallas_call(
        functools.partial(_dispatch_body, bm=bm, n_experts=n_experts),
        grid=(n // tm,),
        in_specs=[smem((tm * TOP_K,), lambda i: (i,)),
                  smem((n_experts,), lambda i: (0,)),
                  smem((n_experts,), lambda i: (0,)),
                  smem((1,), lambda i: (0,)),
                  pl.BlockSpec((tm, c), lambda i: (i, 0))],
        out_specs=pl.BlockSpec(memory_space=pl.ANY),
        out_shape=jax.ShapeDtypeStruct((n_rows, c), h2p.dtype),
        scratch_shapes=[pltpu.VMEM((bm // 2, c), h2p.dtype),
                        pltpu.SemaphoreType.DMA(()), pltpu.SemaphoreType.DMA(())],
        compiler_params=_params("arbitrary", row_dma_loop=True),
        name="dispatch",
    )(dest_flat, pad_lo, pad_n, n_valid, h2p)


def _first_of_expert(be_ref, nv_ref, i):
    changed = jnp.logical_or(i == 0, be_ref[i] != be_ref[jnp.maximum(i - 1, 0)])
    return jnp.logical_and(i < nv_ref[0], changed)


def _moe_up_body(be_ref, nv_ref, xs_ref, w1_ref, b1_ref, o_ref, w1b_scr, xb_scr):
    i = pl.program_id(0)
    j = pl.program_id(1)
    bm, tn = o_ref.shape
    valid = i < nv_ref[0]

    @pl.when(_first_of_expert(be_ref, nv_ref, i))
    def _():
        w1b_scr[j] = w1_ref[0].astype(_BF16)

    @pl.when(jnp.logical_and(valid, j == 0))
    def _():
        xb_scr[...] = xs_ref[...].astype(_BF16)

    @pl.when(valid)
    def _():
        h = _dot(xb_scr[...], w1b_scr[j]) + b1_ref[0]
        even = (lax.broadcasted_iota(jnp.int32, (bm, LANES), 1) & 1) == 0
        for g in range(tn // LANES):
            pa = h[:, (2 * g) * LANES:(2 * g + 1) * LANES]
            pb = h[:, (2 * g + 1) * LANES:(2 * g + 2) * LANES]
            glu = jnp.where(even, pa, pltpu.roll(pb, 1, 1))
            lin = jnp.where(even, pltpu.roll(pa, LANES - 1, 1), pb)
            glu = jnp.minimum(glu, SWIGLU_LIMIT)
            lin = jnp.clip(lin, -SWIGLU_LIMIT, SWIGLU_LIMIT)
            o_ref[:, g * LANES:(g + 1) * LANES] = (
                glu * _sigmoid(SWIGLU_ALPHA * glu) * (lin + 1.0)).astype(o_ref.dtype)

    @pl.when(jnp.logical_not(valid))
    def _():
        o_ref[...] = jnp.zeros_like(o_ref)


def _moe_down_body(be_ref, nv_ref, act_ref, w2_ref, b2_ref, y_ref, w2b_scr, perm_scr):
    i = pl.program_id(0)
    j = pl.program_id(1)
    _, f, tn = w2b_scr.shape
    valid = i < nv_ref[0]
    half = LANES // 2

    @pl.when(_first_of_expert(be_ref, nv_ref, i))
    def _():
        for cb in range(tn // LANES):
            cols = slice(cb * LANES, (cb + 1) * LANES)
            for g in range(f // LANES):
                base = g * LANES
                perm_scr[pl.ds(base, half, stride=2), :] = w2_ref[0, pl.ds(base, half), cols]
                perm_scr[pl.ds(base + 1, half, stride=2), :] = w2_ref[0, pl.ds(base + half, half), cols]
            w2b_scr[j, :, cols] = perm_scr[...].astype(_BF16)

    @pl.when(valid)
    def _():
        y_ref[...] = _dot(act_ref[...], w2b_scr[j]) + b2_ref[0]

    @pl.when(jnp.logical_not(valid))
    def _():
        y_ref[...] = jnp.zeros_like(y_ref)


def _moe_block_rows(assignments, n_experts):
    bm = 128
    while bm < 512 and bm * 2 * 4 <= assignments // n_experts:
        bm *= 2
    return bm


def _moe(xs, block_e, n_valid, w1, b1, w2, b2, bm):
    p, d = xs.shape
    e, _, f2 = w1.shape
    f = f2 // 2
    n_blocks = p // bm
    tn_up = _tile(f, 512, LANES)
    tn_dn = _tile(d, 1024, LANES)
    j_up, j_dn = f // tn_up, d // tn_dn

    def blk(i, nv):
        return jnp.minimum(i, nv[0] - 1)

    def tile_idx(n_j):
        return lambda i, j, be, nv: (be[blk(i, nv)], 0, jnp.where(_first_of_expert(be, nv, i), j, n_j - 1))

    act = pl.pallas_call(
        _moe_up_body,
        grid_spec=pltpu.PrefetchScalarGridSpec(
            num_scalar_prefetch=2,
            grid=(n_blocks, j_up),
            in_specs=[
                pl.BlockSpec((bm, d), lambda i, j, be, nv: (blk(i, nv), 0)),
                pl.BlockSpec((1, d, 2 * tn_up), tile_idx(j_up)),
                pl.BlockSpec((1, 1, 2 * tn_up), lambda i, j, be, nv: (be[blk(i, nv)], 0, j)),
            ],
            out_specs=pl.BlockSpec((bm, tn_up), lambda i, j, be, nv: (i, j)),
            scratch_shapes=[pltpu.VMEM((j_up, d, 2 * tn_up), _BF16), pltpu.VMEM((bm, d), _BF16)],
        ),
        out_shape=jax.ShapeDtypeStruct((p, f), _BF16),
        compiler_params=_params("arbitrary", "arbitrary"),
        name="moe_up",
    )(block_e, n_valid, xs, w1, b1.reshape(e, 1, f2))

    return pl.pallas_call(
        _moe_down_body,
        grid_spec=pltpu.PrefetchScalarGridSpec(
            num_scalar_prefetch=2,
            grid=(n_blocks, j_dn),
            in_specs=[
                pl.BlockSpec((bm, f), lambda i, j, be, nv: (blk(i, nv), 0)),
                pl.BlockSpec((1, f, tn_dn), tile_idx(j_dn)),
                pl.BlockSpec((1, 1, tn_dn), lambda i, j, be, nv: (be[blk(i, nv)], 0, j)),
            ],
            out_specs=pl.BlockSpec((bm, tn_dn), lambda i, j, be, nv: (i, j)),
            scratch_shapes=[pltpu.VMEM((j_dn, f, tn_dn), _BF16), pltpu.VMEM((f, LANES), _F32)],
        ),
        out_shape=jax.ShapeDtypeStruct((p, d), _F32),
        compiler_params=_params("arbitrary", "arbitrary"),
        name="moe_down",
    )(block_e, n_valid, act, w2, b2.reshape(e, 1, d))


def _combine_body(dest_ref, gate_ref, x1_ref, fw_ref, y_hbm, o_ref, y_scr, sem):
    tm, d = x1_ref.shape

    def issue(t, carry):
        for k in range(TOP_K):
            pltpu.make_async_copy(y_hbm.at[pl.ds(dest_ref[t * TOP_K + k], 1), :],
                                  y_scr.at[pl.ds(k * tm + t, 1), :], sem).start()
        return carry

    lax.fori_loop(0, tm, issue, 0)
    pltpu.make_async_copy(y_hbm.at[pl.ds(0, TOP_K * tm), :], y_scr, sem).wait()
    acc = x1_ref[...]
    gates = gate_ref[...]
    for k in range(TOP_K):
        acc = acc + y_scr[pl.ds(k * tm, tm), :] * gates[:, k:k + 1]
    o_ref[...] = _rms_scale(acc) * fw_ref[...]


def _combine(dest_flat, gates, x1, final_gain, y):
    n, d = x1.shape
    tm = _tile(n, 512, 8)
    return pl.pallas_call(
        _combine_body,
        grid=(n // tm,),
        in_specs=[pl.BlockSpec((tm * TOP_K,), lambda i: (i,), memory_space=pltpu.SMEM),
                  pl.BlockSpec((tm, TOP_K), lambda i: (i, 0)),
                  pl.BlockSpec((tm, d), lambda i: (i, 0)),
                  pl.BlockSpec((1, d), lambda i: (0, 0)),
                  pl.BlockSpec(memory_space=pl.ANY)],
        out_specs=pl.BlockSpec((tm, d), lambda i: (i, 0)),
        out_shape=jax.ShapeDtypeStruct((n, d), _F32),
        scratch_shapes=[pltpu.VMEM((TOP_K * tm, d), _F32), pltpu.SemaphoreType.DMA(())],
        compiler_params=_params("arbitrary", row_dma_loop=True),
        name="combine",
    )(dest_flat, gates, x1, final_gain, y)


def kernel(x, mix_norm, w_in, ln_v_w, ln_v_b, w_spatial, b_spatial, gmlp_out_norm, hgrn_lb_fwd,
           hgrn_lb_bwd, hgrn_out_norm, w_out, moe_norm, w_router, b_router, w1, b1, w2, b2, final_norm):
    batch, seq, d = x.shape
    depth = mix_norm.shape[0]
    assert depth == 1, "single trunk layer"
    layer = 0
    d_a = ln_v_w.shape[1]
    d_b = w_out.shape[1] - d_a
    f_b = hgrn_lb_fwd.shape[1]
    n_experts = w_router.shape[2]
    n = batch * seq
    assert seq % GMLP_CHUNK == 0 and seq % SCAN_CHUNK == 0 and d_a % LANES == 0 and d_a == d_b == f_b
    assert hgrn_out_norm.shape[1] == LANES and w_spatial.shape[2] == GMLP_CHUNK

    x2 = x.reshape(n, d)
    z = _in_proj(x2, mix_norm[layer][None], w_in[layer].astype(_BF16))
    a = _gmlp(z, ln_v_w[layer][None], ln_v_b[layer][None], w_spatial[layer].astype(_BF16),
              b_spatial[layer][:, :, None], gmlp_out_norm[layer][None], d_a)
    bo = _hgrn(z, hgrn_lb_fwd, hgrn_lb_bwd, hgrn_out_norm[layer][None], batch, seq, 2 * d_a, f_b, d_b, layer)
    w_out_bf = w_out[layer].astype(_BF16)
    x1, h2p, logits = _out_proj(a, bo, w_out_bf[:d_a], w_out_bf[d_a:], x2, moe_norm[layer][None],
                                w_router[layer].astype(_BF16), b_router[layer][None])

    idx, gates, rank, counts = _route(logits)

    assignments = n * TOP_K
    bm = _moe_block_rows(assignments, n_experts)
    n_blocks = -(-assignments // bm) + n_experts
    counts = counts[0]
    padded = (counts + bm - 1) // bm * bm
    pad_end = jnp.cumsum(padded)
    start_pad = pad_end - padded
    dest = (start_pad[idx] + rank).astype(jnp.int32).reshape(-1)
    block_start = jnp.arange(n_blocks, dtype=jnp.int32) * bm
    block_e = jnp.minimum(jnp.sum(block_start[:, None] >= pad_end[None, :], axis=1),
                          n_experts - 1).astype(jnp.int32)
    n_valid = (pad_end[-1:] // bm).astype(jnp.int32)

    xs = _dispatch(h2p, dest, (start_pad + counts).astype(jnp.int32), (padded - counts).astype(jnp.int32),
                   n_valid, n_blocks * bm, bm)
    y = _moe(xs, block_e, n_valid, w1[layer], b1[layer], w2[layer], b2[layer], bm)
    out = _combine(dest, gates, x1, final_norm[None], y)
    return out.reshape(batch, seq, d)
```
